```python
import math
import jax
import jax.numpy as jnp
from jax import lax
import numpy as np

D_MODEL = 1024
BATCH = 8
SEQ = 2048
DEPTH = 4

GRID_W = 64
CTX_LEN = 256
N_MIXERS = 4
DIFF_MIXER = 0
NA_MIXER = 1
GQA_MIXER = 2
CONV_MIXER = 3
HEAD_DIM = 64
ATTN_SCALE = HEAD_DIM ** -0.5
DIFF_HEADS = D_MODEL // (2 * HEAD_DIM)
NA_HEADS = D_MODEL // HEAD_DIM
NA_WR = 8
NA_WC = 16
NA_QC = 16
NA_KC = NA_QC + NA_WC
GQA_Q_HEADS = D_MODEL // HEAD_DIM
GQA_KV_HEADS = 4
GQA_GROUP = GQA_Q_HEADS // GQA_KV_HEADS
ROPE_THETA = 10000.0
Q_BLOCK = 128
CONV_WIDTH = 3
N_EXPERTS = 16
N_GROUPS = 4
GROUP_SIZE = N_EXPERTS // N_GROUPS
GROUP_SCORE_TOPK = 2
TOPK_GROUPS = 1
TOP_K = 2
EXPERT_FF = 512
NORM_EPS = 1e-6
NEG_INF = -1e30

kernel_name = 'hybrid_interleaved_diffusion_trunk_shared_router_moe'


def rms_norm(x, g):
    xf = x.astype(jnp.float32)
    y = xf * lax.rsqrt(jnp.mean(xf * xf, axis=-1, keepdims=True) + NORM_EPS)
    return (y * g.astype(jnp.float32)).astype(x.dtype)


def modulate(h, shift, scale):
    return h * (1.0 + scale) + shift


def axial_rope(n):
    t = jnp.arange(n)
    row = (t // GRID_W).astype(jnp.float32)
    col = (t % GRID_W).astype(jnp.float32)
    quarter = HEAD_DIM // 4
    inv_freq = ROPE_THETA ** (-jnp.arange(quarter, dtype=jnp.float32) / quarter)
    ang = jnp.concatenate([row[:, None] * inv_freq, col[:, None] * inv_freq], axis=-1)
    return jnp.cos(ang), jnp.sin(ang)


def apply_rope(x, cos, sin):
    x1, x2 = jnp.split(x, 2, axis=-1)
    cos = cos.astype(x.dtype)
    sin = sin.astype(x.dtype)
    return jnp.concatenate([x1 * cos - x2 * sin, x2 * cos + x1 * sin], axis=-1)


def sweep_query_blocks(block_fn, q):
    *lead, n, d = q.shape
    nb = n // Q_BLOCK
    qb = jnp.moveaxis(q.reshape(*lead, nb, Q_BLOCK, d), -3, 0)
    out = lax.map(block_fn, qb)
    out = jnp.moveaxis(out, 0, -3)
    return out.reshape(*out.shape[:-3], n, out.shape[-1])


def softmax_attend(q, k, v):
    s = jnp.einsum('bhgqd,bhkd->bhgqk', q, k).astype(jnp.float32)
    p = jax.nn.softmax(s, axis=-1).astype(v.dtype)
    return jnp.einsum('bhgqk,bhkd->bhgqd', p, v)


def diff_attend(q, k, v, lam):
    s = jnp.einsum('cbhqd,cbhkd->cbhqk', q, k).astype(jnp.float32)
    p = jax.nn.softmax(s, axis=-1)
    a = (p[0] - lam * p[1]).astype(v.dtype)
    return jnp.einsum('bhqk,bhkd->bhqd', a, v)


def diff_attention(h_lat, h_ctx, w_qkv, w_o, lq1, lk1, lq2, lk2, subln_g, lambda_init, cos, sin, ctx_out):
    f32 = jnp.float32
    lam = (jnp.exp(jnp.sum(lq1.astype(f32) * lk1.astype(f32)))
           - jnp.exp(jnp.sum(lq2.astype(f32) * lk2.astype(f32))) + lambda_init)

    def project(h):
        b, n, _ = h.shape
        q, k, v = jnp.split(h @ w_qkv, 3, axis=-1)
        q = q.reshape(b, n, DIFF_HEADS, 2, HEAD_DIM).transpose(3, 0, 2, 1, 4) * ATTN_SCALE
        k = k.reshape(b, n, DIFF_HEADS, 2, HEAD_DIM).transpose(3, 0, 2, 1, 4)
        v = v.reshape(b, n, DIFF_HEADS, 2 * HEAD_DIM).transpose(0, 2, 1, 3)
        return q, k, v

    ql, kl, vl = project(h_lat)
    qc, kc, vc = project(h_ctx)
    ql = apply_rope(ql, cos, sin)
    kl = apply_rope(kl, cos, sin)
    k_all = jnp.concatenate([kc, kl], axis=-2)
    v_all = jnp.concatenate([vc, vl], axis=-2)

    def finish(o):
        o = rms_norm(o, subln_g) * (1.0 - lambda_init)
        b, _, n, _ = o.shape
        return o.transpose(0, 2, 1, 3).reshape(b, n, D_MODEL) @ w_o

    y_lat = finish(sweep_query_blocks(lambda qb: diff_attend(qb, k_all, v_all, lam), ql))
    y_ctx = finish(diff_attend(qc, kc, vc, lam)) if ctx_out else None
    return y_lat, y_ctx


def neighborhood_attention(h_lat, h_ctx, w_qkv, w_o, rpb, ctx_out):
    b, n, _ = h_lat.shape
    rows = n // GRID_W
    wr = min(NA_WR, rows)
    n_cb = GRID_W // NA_QC

    def project(h):
        q, k, v = jnp.split(h @ w_qkv, 3, axis=-1)

        def heads(t):
            return t.reshape(b, t.shape[1], NA_HEADS, HEAD_DIM).transpose(0, 2, 1, 3)
        return heads(q) * ATTN_SCALE, heads(k), heads(v)

    q, k, v = project(h_lat)
    qc, kc, vc = project(h_ctx)
    qg = q.reshape(b, NA_HEADS, rows, GRID_W, HEAD_DIM)
    kg = k.reshape(b, NA_HEADS, rows, GRID_W, HEAD_DIM)
    vg = v.reshape(b, NA_HEADS, rows, GRID_W, HEAD_DIM)

    q_cols = np.arange(GRID_W).reshape(n_cb, NA_QC)
    win_c0 = np.clip(q_cols - NA_WC // 2, 0, GRID_W - NA_WC)
    key_c0 = np.clip(q_cols[:, 0] - NA_WC // 2, 0, GRID_W - NA_KC)
    key_cols = key_c0[:, None] + np.arange(NA_KC)
    kcol = key_cols[:, None, :]
    col_valid = (kcol >= win_c0[..., None]) & (kcol < win_c0[..., None] + NA_WC)
    col_off = np.clip(kcol - q_cols[..., None] + NA_WC - 1, 0, 2 * NA_WC - 2)
    valid = np.broadcast_to(col_valid[:, :, None, :], (n_cb, NA_QC, wr, NA_KC)).reshape(n_cb, NA_QC, wr * NA_KC)
    n_loc = wr * NA_KC

    def row_fn(r):
        rs = jnp.clip(r - wr // 2, 0, rows - wr)
        q_r = lax.dynamic_index_in_dim(qg, r, axis=2, keepdims=False).reshape(b, NA_HEADS, n_cb, NA_QC, HEAD_DIM)

        def gather(t):
            band = lax.dynamic_slice_in_dim(t, rs, wr, axis=2)
            blk = band[:, :, :, key_cols]
            return blk.transpose(0, 1, 3, 2, 4, 5).reshape(b, NA_HEADS, n_cb, n_loc, HEAD_DIM)

        k_blk = gather(kg)
        v_blk = gather(vg)
        row_idx = rs + jnp.arange(wr) - r + (NA_WR - 1)
        bias = jnp.take(rpb, row_idx, axis=1)[:, :, col_off]
        bias = bias.transpose(0, 2, 3, 1, 4).reshape(NA_HEADS, n_cb, NA_QC, n_loc).astype(jnp.float32)
        bias = jnp.where(valid, bias, NEG_INF)
        s_loc = jnp.einsum('bhjqd,bhjkd->bhjqk', q_r, k_blk).astype(jnp.float32) + bias
        s_ctx = jnp.einsum('bhjqd,bhkd->bhjqk', q_r, kc).astype(jnp.float32)
        p = jax.nn.softmax(jnp.concatenate([s_loc, s_ctx], axis=-1), axis=-1).astype(v.dtype)
        o = (jnp.einsum('bhjqk,bhjkd->bhjqd', p[..., :n_loc], v_blk)
             + jnp.einsum('bhjqk,bhkd->bhjqd', p[..., n_loc:], vc))
        return o.reshape(b, NA_HEADS, GRID_W, HEAD_DIM)

    o = lax.map(row_fn, jnp.arange(rows))
    y_lat = o.transpose(1, 0, 3, 2, 4).reshape(b, n, D_MODEL) @ w_o
    y_ctx = None
    if ctx_out:
        oc = softmax_attend(qc[:, :, None], kc, vc)[:, :, 0]
        y_ctx = oc.transpose(0, 2, 1, 3).reshape(b, -1, D_MODEL) @ w_o
    return y_lat, y_ctx


def gqa_attention(h_lat, h_ctx, w_qkv, w_o, q_norm_g, k_norm_g, cos, sin, ctx_out):
    q_w = GQA_Q_HEADS * HEAD_DIM
    kv_w = GQA_KV_HEADS * HEAD_DIM

    def project(h):
        b, n, _ = h.shape
        q, k, v = jnp.split(h @ w_qkv, [q_w, q_w + kv_w], axis=-1)
        q = rms_norm(q.reshape(b, n, GQA_KV_HEADS, GQA_GROUP, HEAD_DIM), q_norm_g).transpose(0, 2, 3, 1, 4) * ATTN_SCALE
        k = rms_norm(k.reshape(b, n, GQA_KV_HEADS, HEAD_DIM), k_norm_g).transpose(0, 2, 1, 3)
        v = v.reshape(b, n, GQA_KV_HEADS, HEAD_DIM).transpose(0, 2, 1, 3)
        return q, k, v

    ql, kl, vl = project(h_lat)
    qc, kc, vc = project(h_ctx)
    ql = apply_rope(ql, cos, sin)
    kl = apply_rope(kl, cos, sin)
    k_all = jnp.concatenate([kc, kl], axis=2)
    v_all = jnp.concatenate([vc, vl], axis=2)

    def merge(o):
        b, _, _, n, _ = o.shape
        return o.transpose(0, 3, 1, 2, 4).reshape(b, n, q_w) @ w_o

    y_lat = merge(sweep_query_blocks(lambda qb: softmax_attend(qb, k_all, v_all), ql))
    y_ctx = merge(softmax_attend(qc, kc, vc)) if ctx_out else None
    return y_lat, y_ctx


def short_conv_mixer(h, w_in, w_conv, w_out):
    gb, gc, u = jnp.split(h @ w_in, 3, axis=-1)
    z = lax.conv_general_dilated(gc * u, w_conv[:, None, :], window_strides=(1,),
                                 padding=((CONV_WIDTH // 2, CONV_WIDTH // 2),),
                                 dimension_numbers=('NWC', 'WIO', 'NWC'),
                                 feature_group_count=D_MODEL)
    return (gb * z) @ w_out


def moe_ffn(h, router_w, router_bias, w_gate, w_up, w_down):
    t = h.shape[0]
    scores = jax.nn.sigmoid((h @ router_w).astype(jnp.float32))
    sel = scores + router_bias.astype(jnp.float32)
    grp_score = lax.top_k(sel.reshape(t, N_GROUPS, GROUP_SIZE), GROUP_SCORE_TOPK)[0].sum(-1)
    _, grp_idx = lax.top_k(grp_score, TOPK_GROUPS)
    grp_mask = jax.nn.one_hot(grp_idx, N_GROUPS, dtype=jnp.float32).sum(1) > 0
    exp_mask = jnp.repeat(grp_mask, GROUP_SIZE, axis=-1)
    _, exp_idx = lax.top_k(jnp.where(exp_mask, sel, NEG_INF), TOP_K)
    w = jnp.take_along_axis(scores, exp_idx, axis=-1)
    w = w / jnp.sum(w, axis=-1, keepdims=True)
    gates = jnp.einsum('tk,tke->te', w, jax.nn.one_hot(exp_idx, N_EXPERTS, dtype=jnp.float32)).astype(h.dtype)
    y = jnp.zeros_like(h)
    for e in range(N_EXPERTS):
        a = jax.nn.silu(h @ w_gate[e]) * (h @ w_up[e])
        y = y + gates[:, e:e + 1] * (a @ w_down[e])
    return y


def setup_inputs(seed: int = 0) -> dict:
    key = jax.random.key(seed)
    keys = iter(jax.random.split(key, 64))

    def nrm(shape, scale):
        return jax.random.normal(next(keys), shape, jnp.float32) * scale

    def gain(shape):
        return 1.0 + nrm(shape, 0.02)

    n_a, n_b, n_c, n_d = (len(range(m, DEPTH, N_MIXERS)) for m in range(N_MIXERS))
    d = D_MODEL
    fan = d ** -0.5
    return {
        'x': nrm((BATCH, SEQ, d), 1.0),
        'c': nrm((BATCH, d), 1.0),
        'ctx': nrm((BATCH, CTX_LEN, d), 1.0),
        'c_ctx': nrm((d,), 1.0),
        'ada_w': nrm((DEPTH, d, 6 * d), 0.5 * fan),
        'ada_b': nrm((DEPTH, 6 * d), 0.02),
        'norm1_g': gain((DEPTH, d)),
        'norm2_g': gain((DEPTH, d)),
        'final_norm_g': gain((d,)),
        'diff_w_qkv': nrm((n_a, d, 3 * d), fan),
        'diff_w_o': nrm((n_a, d, d), fan),
        'diff_lambda_q1': nrm((n_a, HEAD_DIM), 0.1),
        'diff_lambda_k1': nrm((n_a, HEAD_DIM), 0.1),
        'diff_lambda_q2': nrm((n_a, HEAD_DIM), 0.1),
        'diff_lambda_k2': nrm((n_a, HEAD_DIM), 0.1),
        'diff_subln_g': gain((n_a, 2 * HEAD_DIM)),
        'na_w_qkv': nrm((n_b, d, 3 * d), fan),
        'na_w_o': nrm((n_b, d, d), fan),
        'na_rpb': nrm((n_b, NA_HEADS, 2 * NA_WR - 1, 2 * NA_WC - 1), 0.1),
        'gqa_w_qkv': nrm((n_c, d, (GQA_Q_HEADS + 2 * GQA_KV_HEADS) * HEAD_DIM), fan),
        'gqa_w_o': nrm((n_c, GQA_Q_HEADS * HEAD_DIM, d), fan),
        'gqa_q_norm_g': gain((n_c, HEAD_DIM)),
        'gqa_k_norm_g': gain((n_c, HEAD_DIM)),
        'conv_w_in': nrm((n_d, d, 3 * d), fan),
        'conv_w': nrm((n_d, CONV_WIDTH, d), CONV_WIDTH ** -0.5),
        'conv_w_out': nrm((n_d, d, d), fan),
        'router_w': nrm((d, N_EXPERTS), fan),
        'router_bias': nrm((N_EXPERTS,), 0.01),
        'moe_w_gate': nrm((DEPTH, N_EXPERTS, d, EXPERT_FF), fan),
        'moe_w_up': nrm((DEPTH, N_EXPERTS, d, EXPERT_FF), fan),
        'moe_w_down': nrm((DEPTH, N_EXPERTS, EXPERT_FF, d), EXPERT_FF ** -0.5),
    }


def reference(x, c, ctx, c_ctx, ada_w, ada_b, norm1_g, norm2_g, final_norm_g,
              diff_w_qkv, diff_w_o, diff_lambda_q1, diff_lambda_k1, diff_lambda_q2, diff_lambda_k2, diff_subln_g,
              na_w_qkv, na_w_o, na_rpb,
              gqa_w_qkv, gqa_w_o, gqa_q_norm_g, gqa_k_norm_g,
              conv_w_in, conv_w, conv_w_out,
              router_w, router_bias, moe_w_gate, moe_w_up, moe_w_down):
    b, n, d = x.shape
    n_ctx = ctx.shape[1]
    cos, sin = axial_rope(n)
    silu_c = jax.nn.silu(c)
    silu_cc = jax.nn.silu(c_ctx)
    reads_ctx = [i % N_MIXERS != CONV_MIXER for i in range(DEPTH)]
    ctx_needed = [any(reads_ctx[i + 1:]) for i in range(DEPTH)]
    for i in range(DEPTH):
        m = i % N_MIXERS
        j = i // N_MIXERS
        ctx_out = ctx_needed[i]
        sh1, sc1, g1, sh2, sc2, g2 = jnp.split((silu_c @ ada_w[i] + ada_b[i])[:, None, :], 6, axis=-1)
        hl = modulate(rms_norm(x, norm1_g[i]), sh1, sc1)
        hc = None
        if reads_ctx[i] or ctx_out:
            csh1, csc1, cg1, csh2, csc2, cg2 = jnp.split(silu_cc @ ada_w[i] + ada_b[i], 6)
            hc = modulate(rms_norm(ctx, norm1_g[i]), csh1, csc1)
        if m == DIFF_MIXER:
            lambda_init = 0.8 - 0.6 * math.exp(-0.3 * i)
            yl, yc = diff_attention(hl, hc, diff_w_qkv[j], diff_w_o[j], diff_lambda_q1[j], diff_lambda_k1[j],
                                    diff_lambda_q2[j], diff_lambda_k2[j], diff_subln_g[j], lambda_init,
                                    cos, sin, ctx_out)
        elif m == NA_MIXER:
            yl, yc = neighborhood_attention(hl, hc, na_w_qkv[j], na_w_o[j], na_rpb[j], ctx_out)
        elif m == GQA_MIXER:
            yl, yc = gqa_attention(hl, hc, gqa_w_qkv[j], gqa_w_o[j], gqa_q_norm_g[j], gqa_k_norm_g[j],
                                   cos, sin, ctx_out)
        else:
            yl = short_conv_mixer(hl, conv_w_in[j], conv_w[j], conv_w_out[j])
            yc = short_conv_mixer(hc, conv_w_in[j], conv_w[j], conv_w_out[j]) if ctx_out else None
        x = x + g1 * yl
        hl = modulate(rms_norm(x, norm2_g[i]), sh2, sc2).reshape(b * n, d)
        if ctx_out:
            ctx = ctx + cg1 * yc
            hc = modulate(rms_norm(ctx, norm2_g[i]), csh2, csc2).reshape(b * n_ctx, d)
            y = moe_ffn(jnp.concatenate([hc, hl], axis=0), router_w, router_bias,
                        moe_w_gate[i], moe_w_up[i], moe_w_down[i])
            ctx = ctx + cg2 * y[: b * n_ctx].reshape(b, n_ctx, d)
            y = y[b * n_ctx:]
        else:
            y = moe_ffn(hl, router_w, router_bias, moe_w_gate[i], moe_w_up[i], moe_w_down[i])
        x = x + g2 * y.reshape(b, n, d)
    return rms_norm(x, final_norm_g)
```

```python
import functools
import math

import numpy as np
import jax
import jax.numpy as jnp
from jax import lax
from jax.experimental import pallas as pl
from jax.experimental.pallas import tpu as pltpu

D_MODEL = 1024
BATCH = 8
SEQ = 2048
DEPTH = 4
GRID_W = 64
CTX_LEN = 256
S_ALL = CTX_LEN + SEQ
N_MIXERS = 4
HEAD_DIM = 64
ATTN_SCALE = HEAD_DIM ** -0.5
DIFF_HEADS = D_MODEL // (2 * HEAD_DIM)
NA_HEADS = D_MODEL // HEAD_DIM
NA_WR = 8
NA_WC = 16
GQA_Q_HEADS = D_MODEL // HEAD_DIM
GQA_KV_HEADS = 4
ROPE_THETA = 10000.0
N_EXPERTS = 16
N_GROUPS = 4
GROUP_SIZE = N_EXPERTS // N_GROUPS
EXPERT_FF = 512
NORM_EPS = 1e-6
NEG_INF = -1e30

LANES = 128
MOD_ROWS = 16
CTX_MOD_ROW = BATCH
F32 = jnp.float32
BF16 = jnp.bfloat16
VMEM_LIMIT = 56 * 1024 * 1024

_NT = (((1,), (1,)), ((), ()))


def _cparams(n_axes):
    return pltpu.CompilerParams(dimension_semantics=("arbitrary",) * n_axes,
                                vmem_limit_bytes=VMEM_LIMIT)


def _dot(a, b):
    return jnp.dot(a, b, preferred_element_type=F32)


def _dot_nt(a, b):
    return lax.dot_general(a, b, _NT, preferred_element_type=F32)


def _sigmoid(x):
    return 1.0 / (1.0 + jnp.exp(-x))


def _row_mod(mb_ref, mc_ref, k, row0, tm, ctx_rows):
    mb = mb_ref[0, k:k + 1, :]
    if ctx_rows == 0:
        return mb
    rows = row0 + lax.broadcasted_iota(jnp.int32, (tm, 1), 0)
    return jnp.where(rows < ctx_rows, mc_ref[0, k:k + 1, :], mb)


def _norm_mod(x, g, shift, scale):
    y = x * lax.rsqrt(jnp.mean(x * x, axis=-1, keepdims=True) + NORM_EPS) * g
    return y * (1.0 + scale) + shift


def _ada_kernel(cc_ref, w_ref, b_ref, o_ref):
    cc = cc_ref[...]
    s = (cc * _sigmoid(cc)).astype(BF16)
    o_ref[0] = _dot(s, w_ref[0].astype(BF16)) + b_ref[0]


def _ada(cc, ada_w, ada_b):
    tn = 1536
    n = 6 * D_MODEL
    return pl.pallas_call(
        _ada_kernel,
        grid=(DEPTH, n // tn),
        in_specs=[pl.BlockSpec((MOD_ROWS, D_MODEL), lambda l, j: (0, 0)),
                  pl.BlockSpec((1, D_MODEL, tn), lambda l, j: (l, 0, j)),
                  pl.BlockSpec((1, 1, tn), lambda l, j: (l, 0, j))],
        out_specs=pl.BlockSpec((1, MOD_ROWS, tn), lambda l, j: (l, 0, j)),
        out_shape=jax.ShapeDtypeStruct((DEPTH, MOD_ROWS, n), F32),
        compiler_params=_cparams(2), name="ada",
    )(cc, ada_w, ada_b.reshape(DEPTH, 1, n))


def _rope(x, cos, sin_signed):
    lane = lax.broadcasted_iota(jnp.int32, (1, LANES), 1)
    first_half = (lane % HEAD_DIM) < (HEAD_DIM // 2)
    partner = jnp.where(first_half, pltpu.roll(x, LANES - HEAD_DIM // 2, 1), pltpu.roll(x, HEAD_DIM // 2, 1))
    return x * cos + partner * sin_signed


def _proj_kernel(*refs, tm, tn, nout, ctx_rows, rope_cols, norm_cols):
    x_ref, mb_ref, mc_ref, g_ref, w_ref = refs[:5]
    rest = list(refs[5:])
    cos_ref = sin_ref = hg_ref = seg_ref = None
    if rope_cols:
        cos_ref, sin_ref = rest[:2]
        rest = rest[2:]
    if norm_cols:
        hg_ref, seg_ref = rest[:2]
        rest = rest[2:]
    (o_ref,) = rest

    row0 = pl.program_id(1) * tm
    shift = _row_mod(mb_ref, mc_ref, 0, row0, tm, ctx_rows)
    scale = _row_mod(mb_ref, mc_ref, 1, row0, tm, ctx_rows)
    hb = _norm_mod(x_ref[0], g_ref[...], shift, scale).astype(BF16)
    for c in range(nout // tn):
        lo = c * tn
        acc = _dot(hb, w_ref[:, lo:lo + tn])
        if lo < norm_cols:
            sq = acc * acc
            sq_hi = sq.astype(BF16)
            sq_lo = (sq - sq_hi.astype(F32)).astype(BF16)
            ss = _dot(sq_hi, seg_ref[...]) + _dot(sq_lo, seg_ref[...])
            acc = acc * lax.rsqrt(ss * (1.0 / HEAD_DIM) + NORM_EPS) * hg_ref[:, lo:lo + tn]
        if lo < rope_cols:
            cos, sin = cos_ref[...], sin_ref[...]
            acc = jnp.concatenate(
                [_rope(acc[:, k * LANES:(k + 1) * LANES], cos, sin) for k in range(tn // LANES)], axis=1)
        o_ref[0, :, lo:lo + tn] = acc.astype(BF16)


def _proj(x, mods, g, w, *, tm, ctx_rows, rope=None, rope_cols=0, head_gain=None, norm_cols=0):
    b, s, d = x.shape
    nout = w.shape[1]
    tn = 256
    assert s % tm == 0 and nout % tn == 0 and rope_cols % tn == 0 and norm_cols % tn == 0
    in_specs = [pl.BlockSpec((1, tm, d), lambda bi, si: (bi, si, 0)),
                pl.BlockSpec((1, 6, d), lambda bi, si: (bi, 0, 0)),
                pl.BlockSpec((1, 6, d), lambda bi, si: (CTX_MOD_ROW, 0, 0)),
                pl.BlockSpec((1, d), lambda bi, si: (0, 0)),
                pl.BlockSpec((d, nout), lambda bi, si: (0, 0))]
    args = [x, mods, mods, g.reshape(1, d), w]
    if rope_cols:
        in_specs += [pl.BlockSpec((tm, LANES), lambda bi, si: (si, 0))] * 2
        args += list(rope)
    if norm_cols:
        seg = np.kron(np.eye(tn // HEAD_DIM), np.ones((HEAD_DIM, HEAD_DIM))).astype(np.float32)
        in_specs += [pl.BlockSpec((1, norm_cols), lambda bi, si: (0, 0)),
                     pl.BlockSpec((tn, tn), lambda bi, si: (0, 0))]
        args += [head_gain, jnp.asarray(seg, BF16)]
    kern = functools.partial(_proj_kernel, tm=tm, tn=tn, nout=nout, ctx_rows=ctx_rows,
                             rope_cols=rope_cols, norm_cols=norm_cols)
    return pl.pallas_call(
        kern, grid=(b, s // tm), in_specs=in_specs,
        out_specs=pl.BlockSpec((1, tm, nout), lambda bi, si: (bi, si, 0)),
        out_shape=jax.ShapeDtypeStruct((b, s, nout), BF16),
        compiler_params=_cparams(2), name="proj",
    )(*args)


def _lane_halves():
    lane = lax.broadcasted_iota(jnp.int32, (1, LANES), 1)
    return lane < HEAD_DIM


def _split_heads(q, lo):
    zero = jnp.zeros_like(q)
    return jnp.concatenate([jnp.where(lo, q, zero), jnp.where(lo, zero, q)], axis=0)


def _softmax_parts(s_list):
    m = functools.reduce(jnp.maximum, [jnp.max(s, axis=-1, keepdims=True) for s in s_list])
    e_list = [jnp.exp(s - m) for s in s_list]
    l = functools.reduce(lambda a, b: a + b, [jnp.sum(e, axis=-1, keepdims=True) for e in e_list])
    return e_list, l


def _diff_attn_kernel(lam_ref, q_ref, k_ref, v_ref, g_ref, o_ref, *, lam_init):
    lv = lam_ref[...]
    lam = (jnp.exp(jnp.sum(lv[0:1] * lv[1:2], axis=-1, keepdims=True))
           - jnp.exp(jnp.sum(lv[2:3] * lv[3:4], axis=-1, keepdims=True)) + lam_init)
    lo = _lane_halves()

    def attend(nk):
        q = q_ref[0]
        zero = jnp.zeros_like(q)
        k = k_ref[0, :nk]
        (e1,), l1 = _softmax_parts([_dot_nt(jnp.where(lo, q, zero), k)])
        (e2,), l2 = _softmax_parts([_dot_nt(jnp.where(lo, zero, q), k)])
        a = e1 * (1.0 / l1) - e2 * (lam / l2)
        o = _dot(a.astype(BF16), v_ref[0, :nk])
        o = o * lax.rsqrt(jnp.mean(o * o, axis=-1, keepdims=True) + NORM_EPS) * g_ref[...] * (1.0 - lam_init)
        o_ref[0] = o.astype(BF16)

    qi = pl.program_id(2)

    @pl.when(qi == 0)
    def _():
        attend(CTX_LEN)

    @pl.when(qi > 0)
    def _():
        attend(S_ALL)


def _diff_attn(qkv, lam_vecs, subln_g, lam_init):
    b = qkv.shape[0]
    tq = CTX_LEN
    h = DIFF_HEADS
    kern = functools.partial(_diff_attn_kernel, lam_init=lam_init)
    return pl.pallas_call(
        kern, grid=(b, h, S_ALL // tq),
        in_specs=[pl.BlockSpec((4, HEAD_DIM), lambda bi, hi, qi: (0, 0)),
                  pl.BlockSpec((1, tq, LANES), lambda bi, hi, qi: (bi, qi, hi)),
                  pl.BlockSpec((1, S_ALL, LANES), lambda bi, hi, qi: (bi, 0, h + hi)),
                  pl.BlockSpec((1, S_ALL, LANES), lambda bi, hi, qi: (bi, 0, 2 * h + hi)),
                  pl.BlockSpec((1, LANES), lambda bi, hi, qi: (0, 0))],
        out_specs=pl.BlockSpec((1, tq, LANES), lambda bi, hi, qi: (bi, qi, hi)),
        out_shape=jax.ShapeDtypeStruct((b, S_ALL, D_MODEL), BF16),
        compiler_params=_cparams(3), name="diff_attn",
    )(lam_vecs, qkv, qkv, qkv, subln_g.reshape(1, LANES))


NA_ROWS = SEQ // GRID_W
NA_BAND = NA_WR * GRID_W


def _na_attn_kernel(q_ref, k_ref, v_ref, bias_ref, o_ref):
    lo = _lane_halves()
    kc = k_ref[0, :CTX_LEN]
    vc = v_ref[0, :CTX_LEN]

    def merge(o2, rows):
        return jnp.where(lo, o2[:rows], o2[rows:])

    q2 = _split_heads(q_ref[0, :CTX_LEN], lo)
    (e,), l = _softmax_parts([_dot_nt(q2, kc)])
    o_ref[0, :CTX_LEN] = merge(_dot(e.astype(BF16), vc) * (1.0 / l), CTX_LEN).astype(BF16)

    def row(r, carry):
        rs = jnp.clip(r - NA_WR // 2, 0, NA_ROWS - NA_WR)
        ri0 = rs - r + (NA_WR - 1)
        q0 = pl.multiple_of(CTX_LEN + r * GRID_W, GRID_W)
        b0 = pl.multiple_of(CTX_LEN + rs * GRID_W, GRID_W)
        q2 = _split_heads(q_ref[0, pl.ds(q0, GRID_W)], lo)
        kb = k_ref[0, pl.ds(b0, NA_BAND)]
        vb = v_ref[0, pl.ds(b0, NA_BAND)]
        bias = jnp.concatenate([bias_ref[0, ri0], bias_ref[1, ri0]], axis=0)
        (e_loc, e_ctx), l = _softmax_parts([_dot_nt(q2, kb) + bias, _dot_nt(q2, kc)])
        o2 = (_dot(e_loc.astype(BF16), vb) + _dot(e_ctx.astype(BF16), vc)) * (1.0 / l)
        o_ref[0, pl.ds(q0, GRID_W)] = merge(o2, GRID_W).astype(BF16)
        return carry

    lax.fori_loop(0, NA_ROWS, row, 0)


def _na_bias_table(rpb):
    qc = np.arange(GRID_W)[:, None]
    kc = np.arange(GRID_W)[None, :]
    win0 = np.clip(qc - NA_WC // 2, 0, GRID_W - NA_WC)
    valid = (kc >= win0) & (kc < win0 + NA_WC)
    col_off = np.clip(kc - qc + NA_WC - 1, 0, 2 * NA_WC - 2)
    row_idx = np.arange(NA_WR)[:, None] + np.arange(NA_WR)[None, :]
    t = rpb.astype(F32)[:, row_idx]
    t = t[:, :, :, col_off]
    t = jnp.where(valid[None, None, None], t, NEG_INF)
    t = t.transpose(0, 1, 3, 2, 4)
    return t.reshape(NA_HEADS, NA_WR, GRID_W, NA_BAND)


def _na_attn(qkv, bias):
    b = qkv.shape[0]
    hp = NA_HEADS // 2
    return pl.pallas_call(
        _na_attn_kernel, grid=(hp, b),
        in_specs=[pl.BlockSpec((1, S_ALL, LANES), lambda hi, bi: (bi, 0, hi)),
                  pl.BlockSpec((1, S_ALL, LANES), lambda hi, bi: (bi, 0, hp + hi)),
                  pl.BlockSpec((1, S_ALL, LANES), lambda hi, bi: (bi, 0, 2 * hp + hi)),
                  pl.BlockSpec((2, NA_WR, GRID_W, NA_BAND), lambda hi, bi: (hi, 0, 0, 0))],
        out_specs=pl.BlockSpec((1, S_ALL, LANES), lambda hi, bi: (bi, 0, hi)),
        out_shape=jax.ShapeDtypeStruct((b, S_ALL, D_MODEL), BF16),
        compiler_params=_cparams(2), name="na_attn",
    )(qkv, qkv, qkv, bias)


GQA_GROUP = GQA_Q_HEADS // GQA_KV_HEADS


def _gqa_attn_kernel(q_ref, k_ref, v_ref, o_ref, *, tq):
    lo = _lane_halves()
    k = k_ref[0]
    v = v_ref[0]
    outs = [[None, None] for _ in range(GQA_GROUP)]
    for j in range(2):
        qs = []
        for g in range(GQA_GROUP):
            head = j * GQA_GROUP + g
            slab, half = divmod(head, 2)
            q = q_ref[0, :, slab * LANES:(slab + 1) * LANES]
            if half != j:
                q = jnp.concatenate([q[:, HEAD_DIM:], q[:, :HEAD_DIM]], axis=1)
            zero = jnp.zeros_like(q)
            qs.append(jnp.where(lo, q, zero) if j == 0 else jnp.where(lo, zero, q))
        (e,), l = _softmax_parts([_dot_nt(jnp.concatenate(qs, axis=0), k)])
        o = _dot(e.astype(BF16), v) * (1.0 / l)
        for g in range(GQA_GROUP):
            head = j * GQA_GROUP + g
            slab, half = divmod(head, 2)
            og = o[g * tq:(g + 1) * tq]
            if half != j:
                og = pltpu.roll(og, HEAD_DIM, 1)
            outs[slab][half] = og
    for slab in range(GQA_GROUP):
        o_ref[0, :, slab * LANES:(slab + 1) * LANES] = jnp.where(lo, outs[slab][0], outs[slab][1]).astype(BF16)


def _gqa_attn(qkv):
    b = qkv.shape[0]
    tq = 256
    qw = GQA_GROUP * 2 * HEAD_DIM
    n_pairs = GQA_KV_HEADS // 2
    kv0 = GQA_Q_HEADS * HEAD_DIM // LANES
    kern = functools.partial(_gqa_attn_kernel, tq=tq)
    return pl.pallas_call(
        kern, grid=(b, n_pairs, SEQ // tq),
        in_specs=[pl.BlockSpec((1, tq, qw), lambda bi, pi, qi: (bi, qi + CTX_LEN // tq, pi)),
                  pl.BlockSpec((1, S_ALL, LANES), lambda bi, pi, qi: (bi, 0, kv0 + pi)),
                  pl.BlockSpec((1, S_ALL, LANES), lambda bi, pi, qi: (bi, 0, kv0 + n_pairs + pi))],
        out_specs=pl.BlockSpec((1, tq, qw), lambda bi, pi, qi: (bi, qi, pi)),
        out_shape=jax.ShapeDtypeStruct((b, SEQ, D_MODEL), BF16),
        compiler_params=_cparams(3), name="gqa_attn",
    )(qkv, qkv, qkv)


def _conv_kernel(gb_ref, gc_ref, u_ref, w_ref, o_ref):
    p = gc_ref[0].astype(F32) * u_ref[0].astype(F32)
    n = p.shape[0]
    rows = lax.broadcasted_iota(jnp.int32, (n, 1), 0)
    prev = jnp.where(rows == 0, 0.0, pltpu.roll(p, 1, 0))
    nxt = jnp.where(rows == n - 1, 0.0, pltpu.roll(p, n - 1, 0))
    w = w_ref[...]
    z = prev * w[0:1] + p * w[1:2] + nxt * w[2:3]
    o_ref[0] = (gb_ref[0].astype(F32) * z).astype(BF16)


def _conv_mix(proj, w_conv):
    b, n, _ = proj.shape
    nb = D_MODEL // LANES
    spec = lambda off: pl.BlockSpec((1, n, LANES), lambda bi, ci: (bi, 0, off + ci))
    return pl.pallas_call(
        _conv_kernel, grid=(b, nb),
        in_specs=[spec(0), spec(nb), spec(2 * nb), pl.BlockSpec((3, LANES), lambda bi, ci: (0, ci))],
        out_specs=pl.BlockSpec((1, n, LANES), lambda bi, ci: (bi, 0, ci)),
        out_shape=jax.ShapeDtypeStruct((b, n, D_MODEL), BF16),
        compiler_params=_cparams(2), name="conv_mix",
    )(proj, proj, proj, w_conv)


def _route(logits_t, bias):
    scores = _sigmoid(logits_t)
    sel = scores + bias
    nt = sel.shape[1]
    idx = lax.broadcasted_iota(jnp.int32, (N_EXPERTS, nt), 0).astype(F32)
    best = gidx = None
    for g in range(N_GROUPS):
        r = [sel[g * GROUP_SIZE + i:g * GROUP_SIZE + i + 1] for i in range(GROUP_SIZE)]
        pair_sums = [r[i] + r[j] for i in range(GROUP_SIZE) for j in range(i + 1, GROUP_SIZE)]
        gs = functools.reduce(jnp.maximum, pair_sums)
        if g == 0:
            best, gidx = gs, jnp.zeros_like(gs)
        else:
            upd = gs > best
            best = jnp.where(upd, gs, best)
            gidx = jnp.where(upd, float(g), gidx)
    in_group = jnp.floor(idx * (1.0 / GROUP_SIZE)) == gidx
    masked = jnp.where(in_group, sel, NEG_INF)
    m1 = jnp.max(masked, axis=0, keepdims=True)
    i1 = jnp.min(jnp.where(masked == m1, idx, float(N_EXPERTS)), axis=0, keepdims=True)
    masked2 = jnp.where(idx == i1, -jnp.inf, masked)
    m2 = jnp.max(masked2, axis=0, keepdims=True)
    i2 = jnp.min(jnp.where(masked2 == m2, idx, float(N_EXPERTS)), axis=0, keepdims=True)
    w1 = jnp.sum(jnp.where(idx == i1, scores, 0.0), axis=0, keepdims=True)
    w2 = jnp.sum(jnp.where(idx == i2, scores, 0.0), axis=0, keepdims=True)
    inv = 1.0 / (w1 + w2)
    return jnp.where(idx == i1, w1 * inv, 0.0) + jnp.where(idx == i2, w2 * inv, 0.0)


def _out_kernel(o_ref, x_ref, mb_ref, mc_ref, g_ref, wo_ref, rwh_ref, rwl_ref, rb_ref,
                xn_ref, h_ref, gt_ref, *, tm, ctx_rows):
    row0 = pl.program_id(1) * tm
    mod = lambda k: _row_mod(mb_ref, mc_ref, k, row0, tm, ctx_rows)
    x = x_ref[0] + mod(2) * _dot(o_ref[0], wo_ref[...])
    xn_ref[0] = x
    h = _norm_mod(x, g_ref[...], mod(3), mod(4))
    hb = h.astype(BF16)
    h_ref[0] = hb
    hl = (h - hb.astype(F32)).astype(BF16)
    logits_t = _dot_nt(rwh_ref[...], hb) + _dot_nt(rwh_ref[...], hl) + _dot_nt(rwl_ref[...], hb)
    gt_ref[0] = _route(logits_t, rb_ref[...])


def _out_proj(o, x, mods, g, w_o, rw_hi, rw_lo, rbias, *, tm, ctx_rows, x_row_off=0):
    b, s, d = o.shape
    assert s % tm == 0 and x_row_off % tm == 0
    xo = x_row_off // tm
    kern = functools.partial(_out_kernel, tm=tm, ctx_rows=ctx_rows)
    full = lambda shape: pl.BlockSpec(shape, lambda bi, si: (0,) * len(shape))
    return pl.pallas_call(
        kern, grid=(b, s // tm),
        in_specs=[pl.BlockSpec((1, tm, d), lambda bi, si: (bi, si, 0)),
                  pl.BlockSpec((1, tm, d), lambda bi, si: (bi, si + xo, 0)),
                  pl.BlockSpec((1, 6, d), lambda bi, si: (bi, 0, 0)),
                  pl.BlockSpec((1, 6, d), lambda bi, si: (CTX_MOD_ROW, 0, 0)),
                  full((1, d)), full((d, d)), full((N_EXPERTS, d)), full((N_EXPERTS, d)), full((N_EXPERTS, 1))],
        out_specs=[pl.BlockSpec((1, tm, d), lambda bi, si: (bi, si, 0)),
                   pl.BlockSpec((1, tm, d), lambda bi, si: (bi, si, 0)),
                   pl.BlockSpec((1, N_EXPERTS, tm), lambda bi, si: (bi, 0, si))],
        out_shape=[jax.ShapeDtypeStruct((b, s, d), F32),
                   jax.ShapeDtypeStruct((b, s, d), BF16),
                   jax.ShapeDtypeStruct((b, N_EXPERTS, s), F32)],
        compiler_params=_cparams(2), name="out_proj",
    )(o, x, mods, mods, g.reshape(1, d), w_o, rw_hi, rw_lo, rbias)


def _moe_kernel(h_ref, x_ref, gates_ref, mb_ref, mc_ref, wg_ref, wu_ref, wd_ref, o_ref, acc_ref,
                *, tm, ctx_rows, tiles_per_sample):
    e = pl.program_id(1)

    @pl.when(e == 0)
    def _():
        acc_ref[...] = jnp.zeros_like(acc_ref)

    h = h_ref[...]
    g = _dot(h, wg_ref[0])
    u = _dot(h, wu_ref[0])
    lane = lax.broadcasted_iota(jnp.int32, (1, N_EXPERTS), 1)
    gate = jnp.sum(jnp.where(lane == e, gates_ref[...], 0.0), axis=-1, keepdims=True)
    a = (g * _sigmoid(g)) * u * gate
    acc_ref[...] += _dot(a.astype(BF16), wd_ref[0])

    @pl.when(e == N_EXPERTS - 1)
    def _():
        row0 = (pl.program_id(0) % tiles_per_sample) * tm
        o_ref[...] = x_ref[...] + _row_mod(mb_ref, mc_ref, 5, row0, tm, ctx_rows) * acc_ref[...]


def _moe(h, x, gates, mods, wg, wu, wd, *, tm, ctx_rows, s):
    t, d = h.shape
    tps = s // tm
    kern = functools.partial(_moe_kernel, tm=tm, ctx_rows=ctx_rows, tiles_per_sample=tps)
    return pl.pallas_call(
        kern, grid=(t // tm, N_EXPERTS),
        in_specs=[pl.BlockSpec((tm, d), lambda i, e: (i, 0)),
                  pl.BlockSpec((tm, d), lambda i, e: (i, 0)),
                  pl.BlockSpec((tm, N_EXPERTS), lambda i, e: (i, 0)),
                  pl.BlockSpec((1, 6, d), lambda i, e: (i // tps, 0, 0)),
                  pl.BlockSpec((1, 6, d), lambda i, e: (CTX_MOD_ROW, 0, 0)),
                  pl.BlockSpec((1, d, EXPERT_FF), lambda i, e: (e, 0, 0)),
                  pl.BlockSpec((1, d, EXPERT_FF), lambda i, e: (e, 0, 0)),
                  pl.BlockSpec((1, EXPERT_FF, d), lambda i, e: (e, 0, 0))],
        out_specs=pl.BlockSpec((tm, d), lambda i, e: (i, 0)),
        out_shape=jax.ShapeDtypeStruct((t, d), F32),
        scratch_shapes=[pltpu.VMEM((tm, d), F32)],
        compiler_params=_cparams(2), name="moe",
    )(h, x, gates, mods, mods, wg, wu, wd)


def _final_norm_kernel(x_ref, g_ref, o_ref):
    x = x_ref[0]
    o_ref[0] = x * lax.rsqrt(jnp.mean(x * x, axis=-1, keepdims=True) + NORM_EPS) * g_ref[...]


def _final_norm(x, g):
    b, s, d = x.shape
    tm = 1024
    return pl.pallas_call(
        _final_norm_kernel, grid=(b, s // tm),
        in_specs=[pl.BlockSpec((1, tm, d), lambda bi, si: (bi, si, 0)),
                  pl.BlockSpec((1, d), lambda bi, si: (0, 0))],
        out_specs=pl.BlockSpec((1, tm, d), lambda bi, si: (bi, si, 0)),
        out_shape=jax.ShapeDtypeStruct((b, s, d), F32),
        compiler_params=_cparams(2), name="final_norm",
    )(x, g.reshape(1, d))


def _rope_tables():
    t = jnp.arange(SEQ)
    row = (t // GRID_W).astype(F32)
    col = (t % GRID_W).astype(F32)
    quarter = HEAD_DIM // 4
    inv_freq = ROPE_THETA ** (-jnp.arange(quarter, dtype=F32) / quarter)
    ang = jnp.concatenate([row[:, None] * inv_freq, col[:, None] * inv_freq], axis=-1)
    cos, sin = jnp.cos(ang), jnp.sin(ang)
    reps = LANES // HEAD_DIM
    cos_l = jnp.tile(jnp.concatenate([cos, cos], axis=-1), (1, reps))
    sin_l = jnp.tile(jnp.concatenate([-sin, sin], axis=-1), (1, reps))
    cos_all = jnp.concatenate([jnp.ones((CTX_LEN, LANES), F32), cos_l], axis=0)
    sin_all = jnp.concatenate([jnp.zeros((CTX_LEN, LANES), F32), sin_l], axis=0)
    return cos_all, sin_all


def _scale_q(w, q_cols):
    return jnp.concatenate([w[:, :q_cols] * ATTN_SCALE, w[:, q_cols:]], axis=1).astype(BF16)


def kernel(x, c, ctx, c_ctx, ada_w, ada_b, norm1_g, norm2_g, final_norm_g, diff_w_qkv, diff_w_o, diff_lambda_q1, diff_lambda_k1, diff_lambda_q2, diff_lambda_k2, diff_subln_g, na_w_qkv, na_w_o, na_rpb, gqa_w_qkv, gqa_w_o, gqa_q_norm_g, gqa_k_norm_g, conv_w_in, conv_w, conv_w_out, router_w, router_bias, moe_w_gate, moe_w_up, moe_w_down):
    b = x.shape[0]
    cc = jnp.zeros((MOD_ROWS, D_MODEL), F32).at[:b].set(c).at[CTX_MOD_ROW].set(c_ctx)
    mods_all = _ada(cc, ada_w, ada_b).reshape(DEPTH, MOD_ROWS, 6, D_MODEL)
    rope = _rope_tables()
    rw_t = router_w.T
    rw_hi = rw_t.astype(BF16)
    rw_lo = (rw_t - rw_hi.astype(F32)).astype(BF16)
    rbias = router_bias.reshape(N_EXPERTS, 1).astype(F32)

    xs = jnp.concatenate([ctx, x], axis=1)
    for i in range(DEPTH):
        m, j = i % N_MIXERS, i // N_MIXERS
        mods = mods_all[i]
        has_ctx = xs.shape[1] == S_ALL
        ctx_rows = CTX_LEN if has_ctx else 0
        ctx_advances = any(k % N_MIXERS != 3 for k in range(i + 1, DEPTH))
        tm_in = 768 if has_ctx else 1024
        if m == 0:
            lam_init = 0.8 - 0.6 * math.exp(-0.3 * i)
            qkv = _proj(xs, mods, norm1_g[i], _scale_q(diff_w_qkv[j], D_MODEL), tm=tm_in, ctx_rows=ctx_rows,
                        rope=rope, rope_cols=2 * D_MODEL)
            lam_vecs = jnp.stack([diff_lambda_q1[j], diff_lambda_k1[j], diff_lambda_q2[j], diff_lambda_k2[j]])
            o = _diff_attn(qkv, lam_vecs, diff_subln_g[j], lam_init)
            w_o = diff_w_o[j]
        elif m == 1:
            qkv = _proj(xs, mods, norm1_g[i], _scale_q(na_w_qkv[j], D_MODEL), tm=tm_in, ctx_rows=ctx_rows)
            o = _na_attn(qkv, _na_bias_table(na_rpb[j]))
            w_o = na_w_o[j]
        elif m == 2:
            qk_cols = (GQA_Q_HEADS + GQA_KV_HEADS) * HEAD_DIM
            gain = jnp.concatenate([jnp.tile(gqa_q_norm_g[j] * ATTN_SCALE, GQA_Q_HEADS),
                                    jnp.tile(gqa_k_norm_g[j], GQA_KV_HEADS)]).reshape(1, qk_cols)
            qkv = _proj(xs, mods, norm1_g[i], gqa_w_qkv[j].astype(BF16), tm=tm_in, ctx_rows=ctx_rows,
                        rope=rope, rope_cols=qk_cols, head_gain=gain, norm_cols=qk_cols)
            o = _gqa_attn(qkv)
            w_o = gqa_w_o[j]
        else:
            proj = _proj(xs, mods, norm1_g[i], conv_w_in[j].astype(BF16), tm=tm_in, ctx_rows=ctx_rows)
            o = _conv_mix(proj, conv_w[j])
            w_o = conv_w_out[j]

        if has_ctx and not ctx_advances:
            tm_out, x_off, out_ctx_rows = CTX_LEN, CTX_LEN, 0
            if o.shape[1] == S_ALL:
                o = o[:, CTX_LEN:]
        elif has_ctx:
            tm_out, x_off, out_ctx_rows = 768, 0, CTX_LEN
        else:
            tm_out, x_off, out_ctx_rows = 1024, 0, 0
        xs, h, gates_t = _out_proj(o, xs, mods, norm2_g[i], w_o.astype(BF16), rw_hi, rw_lo, rbias,
                                   tm=tm_out, ctx_rows=out_ctx_rows, x_row_off=x_off)
        s = xs.shape[1]
        gates = gates_t.transpose(0, 2, 1).reshape(b * s, N_EXPERTS)
        xs = _moe(h.reshape(b * s, D_MODEL), xs.reshape(b * s, D_MODEL), gates, mods,
                  moe_w_gate[i].astype(BF16), moe_w_up[i].astype(BF16), moe_w_down[i].astype(BF16),
                  tm=s // 2, ctx_rows=out_ctx_rows, s=s).reshape(b, s, D_MODEL)
    if xs.shape[1] == S_ALL:
        xs = xs[:, CTX_LEN:]
    return _final_norm(xs, final_norm_g)
```

```python
import functools
import math

import numpy as np
import jax
import jax.numpy as jnp
from jax import lax
from jax.experimental import pallas as pl
from jax.experimental.pallas import tpu as pltpu

D_MODEL = 1024
BATCH = 8
SEQ = 2048
DEPTH = 4
GRID_W = 64
CTX_LEN = 256
S_ALL = CTX_LEN + SEQ
N_MIXERS = 4
HEAD_DIM = 64
ATTN_SCALE = HEAD_DIM ** -0.5
LOG2E = math.log2(math.e)
QK_SCALE = ATTN_SCALE * LOG2E
DIFF_HEADS = D_MODEL // (2 * HEAD_DIM)
NA_HEADS = D_MODEL // HEAD_DIM
NA_WR = 8
NA_WC = 16
GQA_Q_HEADS = D_MODEL // HEAD_DIM
GQA_KV_HEADS = 4
ROPE_THETA = 10000.0
N_EXPERTS = 16
N_GROUPS = 4
GROUP_SIZE = N_EXPERTS // N_GROUPS
EXPERT_FF = 512
NORM_EPS = 1e-6
NEG_INF = -1e30

LANES = 128
SUBLANES = 8
MOD_ROWS = 16
CTX_MOD_ROW = BATCH
H_EXT = D_MODEL + LANES
ROUTE_ROWS = SUBLANES
MOE_TM = 512
F32 = jnp.float32
BF16 = jnp.bfloat16
VMEM_LIMIT = 56 * 1024 * 1024

_NT = (((1,), (1,)), ((), ()))


def _cparams(n_axes):
    return pltpu.CompilerParams(dimension_semantics=("arbitrary",) * n_axes,
                                vmem_limit_bytes=VMEM_LIMIT)


def _dot(a, b):
    return jnp.dot(a, b, preferred_element_type=F32)


def _dot_nt(a, b):
    return lax.dot_general(a, b, _NT, preferred_element_type=F32)


def _sigmoid(x):
    return 1.0 / (1.0 + jnp.exp(-x))


def _row_mod(mb_ref, mc_ref, k, row0, tm, ctx_rows):
    mb = mb_ref[0, k:k + 1, :]
    if ctx_rows == 0:
        return mb
    rows = row0 + lax.broadcasted_iota(jnp.int32, (tm, 1), 0)
    return jnp.where(rows < ctx_rows, mc_ref[0, k:k + 1, :], mb)


def _rms(x, g):
    return x * lax.rsqrt(jnp.mean(x * x, axis=-1, keepdims=True) + NORM_EPS) * g


def _norm_mod(x, g, shift, scale):
    return _rms(x, g) * (1.0 + scale) + shift


def _ada_kernel(cc_ref, w_ref, b_ref, o_ref):
    cc = cc_ref[...]
    s = (cc * _sigmoid(cc)).astype(BF16)
    o_ref[0] = _dot(s, w_ref[0].astype(BF16)) + b_ref[0]


def _ada(cc, ada_w, ada_b):
    tn = 1536
    n = 6 * D_MODEL
    return pl.pallas_call(
        _ada_kernel,
        grid=(DEPTH, n // tn),
        in_specs=[pl.BlockSpec((MOD_ROWS, D_MODEL), lambda l, j: (0, 0)),
                  pl.BlockSpec((1, D_MODEL, tn), lambda l, j: (l, 0, j)),
                  pl.BlockSpec((1, 1, tn), lambda l, j: (l, 0, j))],
        out_specs=pl.BlockSpec((1, MOD_ROWS, tn), lambda l, j: (l, 0, j)),
        out_shape=jax.ShapeDtypeStruct((DEPTH, MOD_ROWS, n), F32),
        compiler_params=_cparams(2), name="ada",
    )(cc, ada_w, ada_b.reshape(DEPTH, 1, n))


def _rope(x, cos, sin_signed):
    lane = lax.broadcasted_iota(jnp.int32, (1, LANES), 1)
    first_half = (lane % HEAD_DIM) < (HEAD_DIM // 2)
    partner = jnp.where(first_half, pltpu.roll(x, LANES - HEAD_DIM // 2, 1), pltpu.roll(x, HEAD_DIM // 2, 1))
    return x * cos + partner * sin_signed


def _proj_kernel(*refs, tm, tn, nout, ctx_rows, q_cols, rope_cols, norm_cols):
    x_ref, mb_ref, mc_ref, g_ref, w_ref = refs[:5]
    rest = list(refs[5:])
    cos_ref = sin_ref = hg_ref = seg_ref = None
    if rope_cols:
        cos_ref, sin_ref = rest[:2]
        rest = rest[2:]
    if norm_cols:
        hg_ref, seg_ref = rest[:2]
        rest = rest[2:]
    (o_ref,) = rest

    row0 = pl.program_id(1) * tm
    shift = _row_mod(mb_ref, mc_ref, 0, row0, tm, ctx_rows)
    scale = _row_mod(mb_ref, mc_ref, 1, row0, tm, ctx_rows)
    hb = _norm_mod(x_ref[0], g_ref[...], shift, scale).astype(BF16)
    for c in range(nout // tn):
        lo = c * tn
        acc = _dot(hb, w_ref[:, lo:lo + tn])
        if lo < q_cols:
            acc = acc * QK_SCALE
        if lo < norm_cols:
            sq = acc * acc
            sq_hi = sq.astype(BF16)
            sq_lo = (sq - sq_hi.astype(F32)).astype(BF16)
            ss = _dot(sq_hi, seg_ref[...]) + _dot(sq_lo, seg_ref[...])
            acc = acc * lax.rsqrt(ss * (1.0 / HEAD_DIM) + NORM_EPS) * hg_ref[:, lo:lo + tn]
        if lo < rope_cols:
            cos, sin = cos_ref[...], sin_ref[...]
            acc = jnp.concatenate(
                [_rope(acc[:, k * LANES:(k + 1) * LANES], cos, sin) for k in range(tn // LANES)], axis=1)
        o_ref[0, :, lo:lo + tn] = acc.astype(BF16)


def _proj(x, mods, g, w, *, tm, ctx_rows, q_cols=0, rope=None, rope_cols=0, head_gain=None, norm_cols=0):
    b, s, d = x.shape
    nout = w.shape[1]
    tn = 256
    assert s % tm == 0 and nout % tn == 0 and rope_cols % tn == 0 and norm_cols % tn == 0 and q_cols % tn == 0
    in_specs = [pl.BlockSpec((1, tm, d), lambda bi, si: (bi, si, 0)),
                pl.BlockSpec((1, 6, d), lambda bi, si: (bi, 0, 0)),
                pl.BlockSpec((1, 6, d), lambda bi, si: (CTX_MOD_ROW, 0, 0)),
                pl.BlockSpec((1, d), lambda bi, si: (0, 0)),
                pl.BlockSpec((d, nout), lambda bi, si: (0, 0))]
    args = [x, mods, mods, g.reshape(1, d), w]
    if rope_cols:
        in_specs += [pl.BlockSpec((tm, LANES), lambda bi, si: (si, 0))] * 2
        args += list(rope)
    if norm_cols:
        seg = np.kron(np.eye(tn // HEAD_DIM), np.ones((HEAD_DIM, HEAD_DIM))).astype(np.float32)
        in_specs += [pl.BlockSpec((1, norm_cols), lambda bi, si: (0, 0)),
                     pl.BlockSpec((tn, tn), lambda bi, si: (0, 0))]
        args += [head_gain, jnp.asarray(seg, BF16)]
    kern = functools.partial(_proj_kernel, tm=tm, tn=tn, nout=nout, ctx_rows=ctx_rows, q_cols=q_cols,
                             rope_cols=rope_cols, norm_cols=norm_cols)
    return pl.pallas_call(
        kern, grid=(b, s // tm), in_specs=in_specs,
        out_specs=pl.BlockSpec((1, tm, nout), lambda bi, si: (bi, si, 0)),
        out_shape=jax.ShapeDtypeStruct((b, s, nout), BF16),
        compiler_params=_cparams(2), name="proj",
    )(*args)


def _lane_halves():
    lane = lax.broadcasted_iota(jnp.int32, (1, LANES), 1)
    return lane < HEAD_DIM


def _head_half(x, lo, half, fill=0.0):
    f = jnp.full_like(x, fill)
    return jnp.where(lo, x, f) if half == 0 else jnp.where(lo, f, x)


def _exp2_parts(s_list):
    m = functools.reduce(jnp.maximum, [jnp.max(s, axis=-1, keepdims=True) for s in s_list])
    return [jnp.exp2(s - m) for s in s_list]


def _pv_sum(o, half):
    l = o[:, HEAD_DIM:HEAD_DIM + 1] if half == 0 else o[:, 0:1]
    return o * (1.0 / l)


def _diff_attn_kernel(lam_ref, q_ref, k_ref, v_ref, g_ref, o_ref, *, lam_init):
    lv = lam_ref[...]
    lam = (jnp.exp(jnp.sum(lv[0:1] * lv[1:2], axis=-1, keepdims=True))
           - jnp.exp(jnp.sum(lv[2:3] * lv[3:4], axis=-1, keepdims=True)) + lam_init)
    lo = _lane_halves()

    def attend(nk):
        q = q_ref[0]
        k = k_ref[0, :nk]
        (e1,) = _exp2_parts([_dot_nt(_head_half(q, lo, 0), k)])
        (e2,) = _exp2_parts([_dot_nt(_head_half(q, lo, 1), k)])
        l1 = jnp.sum(e1, axis=-1, keepdims=True)
        l2 = jnp.sum(e2, axis=-1, keepdims=True)
        a = e1 - e2 * (lam * l1 / l2)
        o = _dot(a.astype(BF16), v_ref[0, :nk]) * (1.0 / l1)
        o_ref[0] = (_rms(o, g_ref[...]) * (1.0 - lam_init)).astype(BF16)

    qi = pl.program_id(2)

    @pl.when(qi == 0)
    def _():
        attend(CTX_LEN)

    @pl.when(qi > 0)
    def _():
        attend(S_ALL)


def _diff_attn(qkv, lam_vecs, subln_g, lam_init):
    b = qkv.shape[0]
    tq = CTX_LEN
    h = DIFF_HEADS
    kern = functools.partial(_diff_attn_kernel, lam_init=lam_init)
    return pl.pallas_call(
        kern, grid=(b, h, S_ALL // tq),
        in_specs=[pl.BlockSpec((4, HEAD_DIM), lambda bi, hi, qi: (0, 0)),
                  pl.BlockSpec((1, tq, LANES), lambda bi, hi, qi: (bi, qi, hi)),
                  pl.BlockSpec((1, S_ALL, LANES), lambda bi, hi, qi: (bi, 0, h + hi)),
                  pl.BlockSpec((1, S_ALL, LANES), lambda bi, hi, qi: (bi, 0, 2 * h + hi)),
                  pl.BlockSpec((1, LANES), lambda bi, hi, qi: (0, 0))],
        out_specs=pl.BlockSpec((1, tq, LANES), lambda bi, hi, qi: (bi, qi, hi)),
        out_shape=jax.ShapeDtypeStruct((b, S_ALL, D_MODEL), BF16),
        compiler_params=_cparams(3), name="diff_attn",
    )(lam_vecs, qkv, qkv, qkv, subln_g.reshape(1, LANES))


NA_ROWS = SEQ // GRID_W
NA_QROWS = 4
NA_QBLK = NA_QROWS * GRID_W
NA_BROWS = NA_QROWS + NA_WR
NA_BKEYS = NA_BROWS * GRID_W
NA_NBLK = NA_ROWS // NA_QROWS
NA_VARIANTS = 3


def _na_band_start(rb):
    return np.clip(NA_QROWS * rb - NA_WR // 2, 0, NA_ROWS - NA_BROWS)


def _na_attn_kernel(q_ref, k_ref, v_ref, bias_ref, o_ref):
    lo = _lane_halves()
    kc = k_ref[0, :CTX_LEN]
    vc = v_ref[0, :CTX_LEN]
    vc_aug = [_head_half(vc, lo, h, 1.0) for h in range(2)]

    q = q_ref[0, :CTX_LEN]
    outs = []
    for h in range(2):
        (e,) = _exp2_parts([_dot_nt(_head_half(q, lo, h), kc)])
        outs.append(_pv_sum(_dot(e.astype(BF16), vc_aug[h]), h))
    o_ref[0, :CTX_LEN] = jnp.where(lo, outs[0], outs[1]).astype(BF16)

    def block(rb, carry):
        s0 = jnp.clip(NA_QROWS * rb - NA_WR // 2, 0, NA_ROWS - NA_BROWS)
        var = jnp.where(rb == 0, 0, jnp.where(rb == NA_NBLK - 1, 2, 1))
        q0 = pl.multiple_of(CTX_LEN + rb * NA_QBLK, NA_QBLK)
        b0 = pl.multiple_of(CTX_LEN + s0 * GRID_W, GRID_W)
        q = q_ref[0, pl.ds(q0, NA_QBLK)]
        kb = k_ref[0, pl.ds(b0, NA_BKEYS)]
        vb = v_ref[0, pl.ds(b0, NA_BKEYS)]
        outs = []
        for h in range(2):
            qh = _head_half(q, lo, h)
            e_loc, e_ctx = _exp2_parts([_dot_nt(qh, kb) + bias_ref[h, var], _dot_nt(qh, kc)])
            o = _dot(e_loc.astype(BF16), _head_half(vb, lo, h, 1.0)) + _dot(e_ctx.astype(BF16), vc_aug[h])
            outs.append(_pv_sum(o, h))
        o_ref[0, pl.ds(q0, NA_QBLK)] = jnp.where(lo, outs[0], outs[1]).astype(BF16)
        return carry

    lax.fori_loop(0, NA_NBLK, block, 0)


def _na_bias_table(rpb):
    qc = np.arange(GRID_W)[:, None]
    kc = np.arange(GRID_W)[None, :]
    win0 = np.clip(qc - NA_WC // 2, 0, GRID_W - NA_WC)
    col_valid = (kc >= win0) & (kc < win0 + NA_WC)
    col_off = np.clip(kc - qc + NA_WC - 1, 0, 2 * NA_WC - 2)
    row_idx = np.zeros((NA_VARIANTS, NA_QROWS, NA_BROWS), np.int32)
    row_valid = np.zeros((NA_VARIANTS, NA_QROWS, NA_BROWS), bool)
    for var, rb in enumerate((0, 1, NA_NBLK - 1)):
        s0 = _na_band_start(rb)
        for a in range(NA_QROWS):
            r = NA_QROWS * rb + a
            rs = np.clip(r - NA_WR // 2, 0, NA_ROWS - NA_WR)
            for bb in range(NA_BROWS):
                kr = s0 + bb
                row_valid[var, a, bb] = rs <= kr < rs + NA_WR
                row_idx[var, a, bb] = np.clip(kr - r + NA_WR - 1, 0, 2 * NA_WR - 2)
    t = rpb.astype(F32)[:, row_idx]
    t = t[..., col_off]
    valid = row_valid[None, :, :, :, None, None] & col_valid[None, None, None, None]
    t = jnp.where(valid, t * LOG2E, NEG_INF)
    t = t.transpose(0, 1, 2, 4, 3, 5)
    return t.reshape(NA_HEADS, NA_VARIANTS, NA_QBLK, NA_BKEYS)


def _na_attn(qkv, bias):
    b = qkv.shape[0]
    hp = NA_HEADS // 2
    return pl.pallas_call(
        _na_attn_kernel, grid=(hp, b),
        in_specs=[pl.BlockSpec((1, S_ALL, LANES), lambda hi, bi: (bi, 0, hi)),
                  pl.BlockSpec((1, S_ALL, LANES), lambda hi, bi: (bi, 0, hp + hi)),
                  pl.BlockSpec((1, S_ALL, LANES), lambda hi, bi: (bi, 0, 2 * hp + hi)),
                  pl.BlockSpec((2, NA_VARIANTS, NA_QBLK, NA_BKEYS), lambda hi, bi: (hi, 0, 0, 0))],
        out_specs=pl.BlockSpec((1, S_ALL, LANES), lambda hi, bi: (bi, 0, hi)),
        out_shape=jax.ShapeDtypeStruct((b, S_ALL, D_MODEL), BF16),
        compiler_params=_cparams(2), name="na_attn",
    )(qkv, qkv, qkv, bias)


GQA_GROUP = GQA_Q_HEADS // GQA_KV_HEADS


def _gqa_attn_kernel(q_ref, k_ref, v_ref, o_ref, *, tq):
    lo = _lane_halves()
    k = k_ref[0]
    v = v_ref[0]
    outs = [[None, None] for _ in range(GQA_GROUP)]
    for j in range(2):
        qs = []
        for g in range(GQA_GROUP):
            slab, half = divmod(j * GQA_GROUP + g, 2)
            q = q_ref[0, :, slab * LANES:(slab + 1) * LANES]
            if half != j:
                q = jnp.concatenate([q[:, HEAD_DIM:], q[:, :HEAD_DIM]], axis=1)
            qs.append(_head_half(q, lo, j))
        (e,) = _exp2_parts([_dot_nt(jnp.concatenate(qs, axis=0), k)])
        o = _pv_sum(_dot(e.astype(BF16), _head_half(v, lo, j, 1.0)), j)
        for g in range(GQA_GROUP):
            slab, half = divmod(j * GQA_GROUP + g, 2)
            og = o[g * tq:(g + 1) * tq]
            if half != j:
                og = pltpu.roll(og, HEAD_DIM, 1)
            outs[slab][half] = og
    for slab in range(GQA_GROUP):
        o_ref[0, :, slab * LANES:(slab + 1) * LANES] = jnp.where(lo, outs[slab][0], outs[slab][1]).astype(BF16)


def _gqa_attn(qkv):
    b = qkv.shape[0]
    tq = 256
    qw = GQA_GROUP * 2 * HEAD_DIM
    n_pairs = GQA_KV_HEADS // 2
    kv0 = GQA_Q_HEADS * HEAD_DIM // LANES
    kern = functools.partial(_gqa_attn_kernel, tq=tq)
    return pl.pallas_call(
        kern, grid=(b, n_pairs, SEQ // tq),
        in_specs=[pl.BlockSpec((1, tq, qw), lambda bi, pi, qi: (bi, qi + CTX_LEN // tq, pi)),
                  pl.BlockSpec((1, S_ALL, LANES), lambda bi, pi, qi: (bi, 0, kv0 + pi)),
                  pl.BlockSpec((1, S_ALL, LANES), lambda bi, pi, qi: (bi, 0, kv0 + n_pairs + pi))],
        out_specs=pl.BlockSpec((1, tq, qw), lambda bi, pi, qi: (bi, qi, pi)),
        out_shape=jax.ShapeDtypeStruct((b, SEQ, D_MODEL), BF16),
        compiler_params=_cparams(3), name="gqa_attn",
    )(qkv, qkv, qkv)


def _conv_kernel(gb_ref, gc_ref, u_ref, w_ref, o_ref):
    p = gc_ref[0].astype(F32) * u_ref[0].astype(F32)
    n = p.shape[0]
    rows = lax.broadcasted_iota(jnp.int32, (n, 1), 0)
    prev = jnp.where(rows == 0, 0.0, pltpu.roll(p, 1, 0))
    nxt = jnp.where(rows == n - 1, 0.0, pltpu.roll(p, n - 1, 0))
    w = w_ref[...]
    z = prev * w[0:1] + p * w[1:2] + nxt * w[2:3]
    o_ref[0] = (gb_ref[0].astype(F32) * z).astype(BF16)


def _conv_mix(proj, w_conv):
    b, n, _ = proj.shape
    nb = D_MODEL // LANES
    spec = lambda off: pl.BlockSpec((1, n, LANES), lambda bi, ci: (bi, 0, off + ci))
    return pl.pallas_call(
        _conv_kernel, grid=(b, nb),
        in_specs=[spec(0), spec(nb), spec(2 * nb), pl.BlockSpec((3, LANES), lambda bi, ci: (0, ci))],
        out_specs=pl.BlockSpec((1, n, LANES), lambda bi, ci: (bi, 0, ci)),
        out_shape=jax.ShapeDtypeStruct((b, n, D_MODEL), BF16),
        compiler_params=_cparams(2), name="conv_mix",
    )(proj, proj, proj, w_conv)


def _route(logits_t, bias):
    scores = _sigmoid(logits_t)
    sel = scores + bias
    nt = sel.shape[1]
    idx = lax.broadcasted_iota(jnp.int32, (N_EXPERTS, nt), 0).astype(F32)
    best = gidx = None
    for g in range(N_GROUPS):
        r = [sel[g * GROUP_SIZE + i:g * GROUP_SIZE + i + 1] for i in range(GROUP_SIZE)]
        pair_sums = [r[i] + r[j] for i in range(GROUP_SIZE) for j in range(i + 1, GROUP_SIZE)]
        gs = functools.reduce(jnp.maximum, pair_sums)
        if g == 0:
            best, gidx = gs, jnp.zeros_like(gs)
        else:
            upd = gs > best
            best = jnp.where(upd, gs, best)
            gidx = jnp.where(upd, float(g), gidx)
    in_group = jnp.floor(idx * (1.0 / GROUP_SIZE)) == gidx
    masked = jnp.where(in_group, sel, NEG_INF)
    m1 = jnp.max(masked, axis=0, keepdims=True)
    i1 = jnp.min(jnp.where(masked == m1, idx, float(N_EXPERTS)), axis=0, keepdims=True)
    masked2 = jnp.where(idx == i1, -jnp.inf, masked)
    m2 = jnp.max(masked2, axis=0, keepdims=True)
    i2 = jnp.min(jnp.where(masked2 == m2, idx, float(N_EXPERTS)), axis=0, keepdims=True)
    w1 = jnp.sum(jnp.where(idx == i1, scores, 0.0), axis=0, keepdims=True)
    w2 = jnp.sum(jnp.where(idx == i2, scores, 0.0), axis=0, keepdims=True)
    inv = 1.0 / (w1 + w2)
    gates = jnp.where(idx == i1, w1 * inv, 0.0) + jnp.where(idx == i2, w2 * inv, 0.0)
    rows = []
    for i in range(GROUP_SIZE):
        rows.append(functools.reduce(
            lambda a, b: a + b,
            [jnp.where(gidx == float(g), gates[g * GROUP_SIZE + i:g * GROUP_SIZE + i + 1], 0.0) for g in range(N_GROUPS)]))
    rows.append(gidx)
    rows += [jnp.zeros_like(gidx)] * (ROUTE_ROWS - len(rows))
    return jnp.concatenate(rows, axis=0)


def _out_kernel(o_ref, x_ref, mb_ref, mc_ref, g_ref, wo_ref, rwh_ref, rwl_ref, rb_ref,
                xn_ref, h_ref, rt_ref, *, tm, ctx_rows):
    row0 = pl.program_id(1) * tm
    mod = lambda k: _row_mod(mb_ref, mc_ref, k, row0, tm, ctx_rows)
    x = x_ref[0] + mod(2) * _dot(o_ref[0], wo_ref[...])
    xn_ref[0] = x
    h = _norm_mod(x, g_ref[...], mod(3), mod(4))
    hb = h.astype(BF16)
    hl = (h - hb.astype(F32)).astype(BF16)
    logits_t = _dot_nt(rwh_ref[...], hb) + _dot_nt(rwh_ref[...], hl) + _dot_nt(rwl_ref[...], hb)
    route = _route(logits_t, rb_ref[...])
    rt_ref[0] = route
    h_ref[0, :, :D_MODEL] = h
    h_ref[0, :, D_MODEL:] = jnp.concatenate([route, jnp.zeros((LANES - ROUTE_ROWS, tm), F32)], axis=0).T


def _out_proj(o, x, mods, g, w_o, rw_hi, rw_lo, rbias, *, tm, ctx_rows, x_row_off=0):
    b, s, d = o.shape
    assert s % tm == 0 and x_row_off % tm == 0
    xo = x_row_off // tm
    kern = functools.partial(_out_kernel, tm=tm, ctx_rows=ctx_rows)
    full = lambda shape: pl.BlockSpec(shape, lambda bi, si: (0,) * len(shape))
    return pl.pallas_call(
        kern, grid=(b, s // tm),
        in_specs=[pl.BlockSpec((1, tm, d), lambda bi, si: (bi, si, 0)),
                  pl.BlockSpec((1, tm, d), lambda bi, si: (bi, si + xo, 0)),
                  pl.BlockSpec((1, 6, d), lambda bi, si: (bi, 0, 0)),
                  pl.BlockSpec((1, 6, d), lambda bi, si: (CTX_MOD_ROW, 0, 0)),
                  full((1, d)), full((d, d)), full((N_EXPERTS, d)), full((N_EXPERTS, d)), full((N_EXPERTS, 1))],
        out_specs=[pl.BlockSpec((1, tm, d), lambda bi, si: (bi, si, 0)),
                   pl.BlockSpec((1, tm, H_EXT), lambda bi, si: (bi, si, 0)),
                   pl.BlockSpec((1, ROUTE_ROWS, tm), lambda bi, si: (bi, 0, si))],
        out_shape=[jax.ShapeDtypeStruct((b, s, d), F32),
                   jax.ShapeDtypeStruct((b, s, H_EXT), F32),
                   jax.ShapeDtypeStruct((b, ROUTE_ROWS, s), F32)],
        compiler_params=_cparams(2), name="out_proj",
    )(o, x, mods, mods, g.reshape(1, d), w_o, rw_hi, rw_lo, rbias)


def _row_copy(src_hbm, buf, sem, slot, src_row, dst_row):
    return pltpu.make_async_copy(src_hbm.at[pl.ds(src_row, 1)], buf.at[slot, pl.ds(dst_row, 1)], sem.at[slot])


def _start_row_gather(idx_ref, tile, src_hbm, buf, sem, n):
    slot = tile % 2
    base = tile * n

    def body(j, carry):
        _row_copy(src_hbm, buf, sem, slot, idx_ref[base + j], j).start()
        return carry

    lax.fori_loop(0, n, body, 0, unroll=8)


def _gathered_rows(idx_ref, src_hbm, buf, sem, n, n_tiles):
    i = pl.program_id(0)

    @pl.when(i == 0)
    def _():
        _start_row_gather(idx_ref, i, src_hbm, buf, sem, n)

    @pl.when(i + 1 < n_tiles)
    def _():
        _start_row_gather(idx_ref, i + 1, src_hbm, buf, sem, n)

    slot = i % 2
    pltpu.make_async_copy(src_hbm.at[pl.ds(0, n)], buf.at[slot], sem.at[slot]).wait()
    return buf.at[slot]


def _moe_kernel(perm_ref, tgroup_ref, nact_ref, h_hbm, wg_ref, wu_ref, wd_ref, o_ref, buf, sem, *, tm, n_tiles):
    rows = _gathered_rows(perm_ref, h_hbm, buf, sem, tm, n_tiles)
    i = pl.program_id(0)

    @pl.when(i < nact_ref[0])
    def _():
        hb = rows[:, :D_MODEL].astype(BF16)
        gates = rows[:, D_MODEL:]
        acc = jnp.zeros((tm, D_MODEL), F32)
        for j in range(GROUP_SIZE):
            g = _dot(hb, wg_ref[j])
            u = _dot(hb, wu_ref[j])
            a = (g * _sigmoid(g)) * u * gates[:, j:j + 1]
            acc = acc + _dot(a.astype(BF16), wd_ref[j])
        o_ref[...] = acc

    @pl.when(i >= nact_ref[0])
    def _():
        o_ref[...] = jnp.zeros_like(o_ref)


def _moe(h_ext, perm, tile_group, n_active, wg, wu, wd):
    tm = MOE_TM
    n_tiles = perm.shape[0] // tm
    d = D_MODEL
    wspec = lambda shape: pl.BlockSpec(shape, lambda i, perm, tg, na: (tg[i], 0, 0))
    kern = functools.partial(_moe_kernel, tm=tm, n_tiles=n_tiles)
    return pl.pallas_call(
        kern,
        grid_spec=pltpu.PrefetchScalarGridSpec(
            num_scalar_prefetch=3, grid=(n_tiles,),
            in_specs=[pl.BlockSpec(memory_space=pl.ANY),
                      wspec((GROUP_SIZE, d, EXPERT_FF)), wspec((GROUP_SIZE, d, EXPERT_FF)),
                      wspec((GROUP_SIZE, EXPERT_FF, d))],
            out_specs=pl.BlockSpec((tm, d), lambda i, perm, tg, na: (i, 0)),
            scratch_shapes=[pltpu.VMEM((2, tm, H_EXT), F32), pltpu.SemaphoreType.DMA((2,))]),
        out_shape=jax.ShapeDtypeStruct((n_tiles * tm, d), F32),
        compiler_params=_cparams(1), name="moe",
    )(perm, tile_group, n_active, h_ext, wg, wu, wd)


def _moe_plan(route, tm):
    b, _, s = route.shape
    t = b * s
    n_tiles = t // tm + N_GROUPS
    gid = route[:, GROUP_SIZE, :].reshape(t).astype(jnp.int32)
    onehot = (gid[:, None] == jnp.arange(N_GROUPS, dtype=jnp.int32)[None, :]).astype(jnp.int32)
    csum = jnp.cumsum(onehot, axis=0)
    counts = csum[-1]
    rank = jnp.sum((csum - onehot) * onehot, axis=1)
    tiles = (counts + tm - 1) // tm
    tile_end = jnp.cumsum(tiles)
    tile_start = tile_end - tiles
    pos = jnp.sum(onehot * tile_start[None, :], axis=1) * tm + rank
    perm = jnp.zeros((n_tiles * tm,), jnp.int32).at[pos].set(jnp.arange(t, dtype=jnp.int32), unique_indices=True)
    tile_ids = jnp.arange(n_tiles, dtype=jnp.int32)
    tile_group = jnp.minimum(jnp.sum((tile_ids[:, None] >= tile_end[None, :]).astype(jnp.int32), axis=1), N_GROUPS - 1)
    return pos.astype(jnp.int32), perm, tile_group.astype(jnp.int32), tile_end[-1:].astype(jnp.int32)


def _combine_kernel(*refs, tm, ctx_rows, tiles_per_sample, n_tiles, final):
    pos_ref, x_ref, y_hbm, mb_ref, mc_ref = refs[:5]
    rest = list(refs[5:])
    fg_ref = rest.pop(0) if final else None
    o_ref, buf, sem = rest
    y = _gathered_rows(pos_ref, y_hbm, buf, sem, tm, n_tiles)
    row0 = (pl.program_id(0) % tiles_per_sample) * tm
    x = x_ref[...] + _row_mod(mb_ref, mc_ref, 5, row0, tm, ctx_rows) * y[...]
    if final:
        x = _rms(x, fg_ref[...])
    o_ref[...] = x


def _combine(x, y_sorted, pos, mods, *, tm, ctx_rows, s, final_g=None):
    t, d = x.shape
    tps = s // tm
    n_tiles = t // tm
    final = final_g is not None
    in_specs = [pl.BlockSpec((tm, d), lambda i, pos: (i, 0)),
                pl.BlockSpec(memory_space=pl.ANY),
                pl.BlockSpec((1, 6, d), lambda i, pos: (i // tps, 0, 0)),
                pl.BlockSpec((1, 6, d), lambda i, pos: (CTX_MOD_ROW, 0, 0))]
    args = [pos, x, y_sorted, mods, mods]
    if final:
        in_specs.append(pl.BlockSpec((1, d), lambda i, pos: (0, 0)))
        args.append(final_g.reshape(1, d))
    kern = functools.partial(_combine_kernel, tm=tm, ctx_rows=ctx_rows, tiles_per_sample=tps, n_tiles=n_tiles, final=final)
    return pl.pallas_call(
        kern,
        grid_spec=pltpu.PrefetchScalarGridSpec(
            num_scalar_prefetch=1, grid=(n_tiles,), in_specs=in_specs,
            out_specs=pl.BlockSpec((tm, d), lambda i, pos: (i, 0)),
            scratch_shapes=[pltpu.VMEM((2, tm, d), F32), pltpu.SemaphoreType.DMA((2,))]),
        out_shape=jax.ShapeDtypeStruct((t, d), F32),
        compiler_params=_cparams(1), name="combine",
    )(*args)


def _rope_tables():
    t = jnp.arange(SEQ)
    row = (t // GRID_W).astype(F32)
    col = (t % GRID_W).astype(F32)
    quarter = HEAD_DIM // 4
    inv_freq = ROPE_THETA ** (-jnp.arange(quarter, dtype=F32) / quarter)
    ang = jnp.concatenate([row[:, None] * inv_freq, col[:, None] * inv_freq], axis=-1)
    cos, sin = jnp.cos(ang), jnp.sin(ang)
    reps = LANES // HEAD_DIM
    cos_l = jnp.tile(jnp.concatenate([cos, cos], axis=-1), (1, reps))
    sin_l = jnp.tile(jnp.concatenate([-sin, sin], axis=-1), (1, reps))
    cos_all = jnp.concatenate([jnp.ones((CTX_LEN, LANES), F32), cos_l], axis=0)
    sin_all = jnp.concatenate([jnp.zeros((CTX_LEN, LANES), F32), sin_l], axis=0)
    return cos_all, sin_all


def kernel(x, c, ctx, c_ctx, ada_w, ada_b, norm1_g, norm2_g, final_norm_g, diff_w_qkv, diff_w_o, diff_lambda_q1, diff_lambda_k1, diff_lambda_q2, diff_lambda_k2, diff_subln_g, na_w_qkv, na_w_o, na_rpb, gqa_w_qkv, gqa_w_o, gqa_q_norm_g, gqa_k_norm_g, conv_w_in, conv_w, conv_w_out, router_w, router_bias, moe_w_gate, moe_w_up, moe_w_down):
    b = x.shape[0]
    cc = jnp.zeros((MOD_ROWS, D_MODEL), F32).at[:b].set(c).at[CTX_MOD_ROW].set(c_ctx)
    mods_all = _ada(cc, ada_w, ada_b).reshape(DEPTH, MOD_ROWS, 6, D_MODEL)
    rope = _rope_tables()
    rw_t = router_w.T
    rw_hi = rw_t.astype(BF16)
    rw_lo = (rw_t - rw_hi.astype(F32)).astype(BF16)
    rbias = router_bias.reshape(N_EXPERTS, 1).astype(F32)
    wg_all, wu_all, wd_all = moe_w_gate.astype(BF16), moe_w_up.astype(BF16), moe_w_down.astype(BF16)

    xs = jnp.concatenate([ctx, x], axis=1)
    for i in range(DEPTH):
        m, j = i % N_MIXERS, i // N_MIXERS
        mods = mods_all[i]
        has_ctx = xs.shape[1] == S_ALL
        ctx_rows = CTX_LEN if has_ctx else 0
        ctx_advances = any(k % N_MIXERS != 3 for k in range(i + 1, DEPTH))
        tm_in = 768 if has_ctx else 1024
        if m == 0:
            lam_init = 0.8 - 0.6 * math.exp(-0.3 * i)
            qkv = _proj(xs, mods, norm1_g[i], diff_w_qkv[j].astype(BF16), tm=tm_in, ctx_rows=ctx_rows,
                        q_cols=D_MODEL, rope=rope, rope_cols=2 * D_MODEL)
            lam_vecs = jnp.stack([diff_lambda_q1[j], diff_lambda_k1[j], diff_lambda_q2[j], diff_lambda_k2[j]])
            o = _diff_attn(qkv, lam_vecs, diff_subln_g[j], lam_init)
            w_o = diff_w_o[j]
        elif m == 1:
            qkv = _proj(xs, mods, norm1_g[i], na_w_qkv[j].astype(BF16), tm=tm_in, ctx_rows=ctx_rows, q_cols=D_MODEL)
            o = _na_attn(qkv, _na_bias_table(na_rpb[j]))
            w_o = na_w_o[j]
        elif m == 2:
            qk_cols = (GQA_Q_HEADS + GQA_KV_HEADS) * HEAD_DIM
            gain = jnp.concatenate([jnp.tile(gqa_q_norm_g[j] * QK_SCALE, GQA_Q_HEADS),
                                    jnp.tile(gqa_k_norm_g[j], GQA_KV_HEADS)]).reshape(1, qk_cols)
            qkv = _proj(xs, mods, norm1_g[i], gqa_w_qkv[j].astype(BF16), tm=tm_in, ctx_rows=ctx_rows,
                        rope=rope, rope_cols=qk_cols, head_gain=gain, norm_cols=qk_cols)
            o = _gqa_attn(qkv)
            w_o = gqa_w_o[j]
        else:
            proj = _proj(xs, mods, norm1_g[i], conv_w_in[j].astype(BF16), tm=tm_in, ctx_rows=ctx_rows)
            o = _conv_mix(proj, conv_w[j])
            w_o = conv_w_out[j]

        if has_ctx and not ctx_advances:
            tm_out, x_off, out_ctx_rows = CTX_LEN, CTX_LEN, 0
            if o.shape[1] == S_ALL:
                o = o[:, CTX_LEN:]
        elif has_ctx:
            tm_out, x_off, out_ctx_rows = 768, 0, CTX_LEN
        else:
            tm_out, x_off, out_ctx_rows = 1024, 0, 0
        xs, h_ext, route = _out_proj(o, xs, mods, norm2_g[i], w_o.astype(BF16), rw_hi, rw_lo, rbias,
                                     tm=tm_out, ctx_rows=out_ctx_rows, x_row_off=x_off)
        s = xs.shape[1]
        pos, perm, tile_group, n_active = _moe_plan(route, MOE_TM)
        y_sorted = _moe(h_ext.reshape(b * s, H_EXT), perm, tile_group, n_active, wg_all[i], wu_all[i], wd_all[i])
        final_g = final_norm_g if (i == DEPTH - 1 and s == SEQ) else None
        xs = _combine(xs.reshape(b * s, D_MODEL), y_sorted, pos, mods, tm=768 if s == S_ALL else 1024,
                      ctx_rows=out_ctx_rows, s=s, final_g=final_g).reshape(b, s, D_MODEL)
    if xs.shape[1] == S_ALL:
        xs = xs[:, CTX_LEN:]
        xs = _combine_final_norm(xs, final_norm_g)
    return xs


def _final_norm_kernel(x_ref, g_ref, o_ref):
    o_ref[0] = _rms(x_ref[0], g_ref[...])


def _combine_final_norm(x, g):
    b, s, d = x.shape
    tm = 1024
    return pl.pallas_call(
        _final_norm_kernel, grid=(b, s // tm),
        in_specs=[pl.BlockSpec((1, tm, d), lambda bi, si: (bi, si, 0)),
                  pl.BlockSpec((1, d), lambda bi, si: (0, 0))],
        out_specs=pl.BlockSpec((1, tm, d), lambda bi, si: (bi, si, 0)),
        out_shape=jax.ShapeDtypeStruct((b, s, d), F32),
        compiler_params=_cparams(2), name="final_norm",
    )(x, g.reshape(1, d))
```

```python
import functools
import math

import numpy as np
import jax
import jax.numpy as jnp
from jax import lax
from jax.experimental import pallas as pl
from jax.experimental.pallas import tpu as pltpu

D_MODEL = 1024
BATCH = 8
SEQ = 2048
DEPTH = 4
GRID_W = 64
CTX_LEN = 256
S_ALL = CTX_LEN + SEQ
N_MIXERS = 4
HEAD_DIM = 64
ATTN_SCALE = HEAD_DIM ** -0.5
LOG2E = math.log2(math.e)
QK_SCALE = ATTN_SCALE * LOG2E
DIFF_HEADS = D_MODEL // (2 * HEAD_DIM)
NA_HEADS = D_MODEL // HEAD_DIM
NA_WR = 8
NA_WC = 16
GQA_Q_HEADS = D_MODEL // HEAD_DIM
GQA_KV_HEADS = 4
ROPE_THETA = 10000.0
N_EXPERTS = 16
N_GROUPS = 4
GROUP_SIZE = N_EXPERTS // N_GROUPS
EXPERT_FF = 512
NORM_EPS = 1e-6
NEG_INF = -1e30

LANES = 128
SUBLANES = 8
MOD_ROWS = 16
CTX_MOD_ROW = BATCH
H_EXT = D_MODEL + LANES
ROUTE_ROWS = SUBLANES
MOE_TM = 512
F32 = jnp.float32
BF16 = jnp.bfloat16
VMEM_LIMIT = 56 * 1024 * 1024

_NT = (((1,), (1,)), ((), ()))


def _cparams(n_axes):
    return pltpu.CompilerParams(dimension_semantics=("arbitrary",) * n_axes,
                                vmem_limit_bytes=VMEM_LIMIT)


def _dot(a, b):
    return jnp.dot(a, b, preferred_element_type=F32)


def _dot_nt(a, b):
    return lax.dot_general(a, b, _NT, preferred_element_type=F32)


def _sigmoid(x):
    return 1.0 / (1.0 + jnp.exp(-x))


def _row_mod(mb_ref, mc_ref, k, row0, tm, ctx_rows):
    mb = mb_ref[0, k:k + 1, :]
    if ctx_rows == 0:
        return mb
    rows = row0 + lax.broadcasted_iota(jnp.int32, (tm, 1), 0)
    return jnp.where(rows < ctx_rows, mc_ref[0, k:k + 1, :], mb)


def _rms(x, g):
    return x * lax.rsqrt(jnp.mean(x * x, axis=-1, keepdims=True) + NORM_EPS) * g


def _norm_mod(x, g, shift, scale):
    return _rms(x, g) * (1.0 + scale) + shift


def _row_copy(src_hbm, buf, sem, slot, src_row, dst_row):
    return pltpu.make_async_copy(src_hbm.at[pl.ds(src_row, 1)], buf.at[slot, pl.ds(dst_row, 1)], sem.at[slot])


def _wait_rows(src_hbm, buf, sem, slot, n):
    pltpu.make_async_copy(src_hbm.at[pl.ds(0, n)], buf.at[slot], sem.at[slot]).wait()


def _start_rows_loop(row_of, src_hbm, buf, sem, slot, n):
    def body(j, carry):
        _row_copy(src_hbm, buf, sem, slot, row_of(j), j).start()
        return carry

    lax.fori_loop(0, n, body, 0, unroll=8)


def _split_rows(n, parts):
    edges = [round(k * n / parts) for k in range(parts + 1)]
    return list(zip(edges[:-1], edges[1:]))


class _RowPrefetch:
    def __init__(self, row_of, src_hbm, buf, sem, slot, n, parts):
        self.args = (row_of, src_hbm, buf, sem, slot)
        self.pieces = _split_rows(n, parts)
        self.k = 0

    def issue_piece(self):
        row_of, src_hbm, buf, sem, slot = self.args
        lo, hi = self.pieces[self.k]
        self.k += 1
        for j in range(lo, hi):
            _row_copy(src_hbm, buf, sem, slot, row_of(j), j).start()


def _ada_kernel(cc_ref, w_ref, b_ref, o_ref):
    cc = cc_ref[...]
    s = (cc * _sigmoid(cc)).astype(BF16)
    o_ref[0] = _dot(s, w_ref[0].astype(BF16)) + b_ref[0]


def _ada(cc, ada_w, ada_b):
    tn = 1536
    n = 6 * D_MODEL
    return pl.pallas_call(
        _ada_kernel,
        grid=(DEPTH, n // tn),
        in_specs=[pl.BlockSpec((MOD_ROWS, D_MODEL), lambda l, j: (0, 0)),
                  pl.BlockSpec((1, D_MODEL, tn), lambda l, j: (l, 0, j)),
                  pl.BlockSpec((1, 1, tn), lambda l, j: (l, 0, j))],
        out_specs=pl.BlockSpec((1, MOD_ROWS, tn), lambda l, j: (l, 0, j)),
        out_shape=jax.ShapeDtypeStruct((DEPTH, MOD_ROWS, n), F32),
        compiler_params=_cparams(2), name="ada",
    )(cc, ada_w, ada_b.reshape(DEPTH, 1, n))


def _rope(x, cos, sin_signed):
    lane = lax.broadcasted_iota(jnp.int32, (1, LANES), 1)
    first_half = (lane % HEAD_DIM) < (HEAD_DIM // 2)
    partner = jnp.where(first_half, pltpu.roll(x, LANES - HEAD_DIM // 2, 1), pltpu.roll(x, HEAD_DIM // 2, 1))
    return x * cos + partner * sin_signed


def _proj_kernel(*refs, tm, tn, nout, ctx_rows, q_cols, rope_cols, norm_cols, fuse_moe, n_s, n_tiles):
    refs = list(refs)
    pos_ref = refs.pop(0) if fuse_moe else None
    x_ref, mb_ref, mc_ref, g_ref, w_ref = refs[:5]
    rest = refs[5:]
    cos_ref = sin_ref = hg_ref = seg_ref = y_hbm = pmb_ref = pmc_ref = None
    if rope_cols:
        cos_ref, sin_ref = rest[:2]
        rest = rest[2:]
    if norm_cols:
        hg_ref, seg_ref = rest[:2]
        rest = rest[2:]
    if fuse_moe:
        y_hbm, pmb_ref, pmc_ref = rest[:3]
        o_ref, xn_ref, buf, sem = rest[3:]
    else:
        (o_ref,) = rest

    row0 = pl.program_id(1) * tm
    x = x_ref[0]
    prefetch = None
    n_chunks = nout // tn
    if fuse_moe:
        lin = pl.program_id(0) * n_s + pl.program_id(1)
        slot = lin % 2

        @pl.when(lin == 0)
        def _():
            _start_rows_loop(lambda j: pos_ref[j], y_hbm, buf, sem, 0, tm)

        _wait_rows(y_hbm, buf, sem, slot, tm)
        x = x + _row_mod(pmb_ref, pmc_ref, 5, row0, tm, ctx_rows) * buf[slot]
        xn_ref[0] = x
        nxt = jnp.minimum(lin + 1, n_tiles - 1) * tm
        prefetch = _RowPrefetch(lambda j: pos_ref[nxt + j], y_hbm, buf, sem, 1 - slot, tm, n_chunks)

    shift = _row_mod(mb_ref, mc_ref, 0, row0, tm, ctx_rows)
    scale = _row_mod(mb_ref, mc_ref, 1, row0, tm, ctx_rows)
    hb = _norm_mod(x, g_ref[...], shift, scale).astype(BF16)
    for c in range(n_chunks):
        lo = c * tn
        acc = _dot(hb, w_ref[:, lo:lo + tn])
        if prefetch is not None:
            prefetch.issue_piece()
        if lo < q_cols:
            acc = acc * QK_SCALE
        if lo < norm_cols:
            sq = acc * acc
            sq_hi = sq.astype(BF16)
            sq_lo = (sq - sq_hi.astype(F32)).astype(BF16)
            ss = _dot(sq_hi, seg_ref[...]) + _dot(sq_lo, seg_ref[...])
            acc = acc * lax.rsqrt(ss * (1.0 / HEAD_DIM) + NORM_EPS) * hg_ref[:, lo:lo + tn]
        if lo < rope_cols:
            cos, sin = cos_ref[...], sin_ref[...]
            acc = jnp.concatenate(
                [_rope(acc[:, k * LANES:(k + 1) * LANES], cos, sin) for k in range(tn // LANES)], axis=1)
        o_ref[0, :, lo:lo + tn] = acc.astype(BF16)

    if fuse_moe:
        @pl.when(lin == n_tiles - 1)
        def _():
            _wait_rows(y_hbm, buf, sem, 1 - slot, tm)


def _proj(x, mods, g, w, *, tm, ctx_rows, q_cols=0, rope=None, rope_cols=0, head_gain=None, norm_cols=0, moe=None):
    b, s, d = x.shape
    nout = w.shape[1]
    tn = 256
    n_s = s // tm
    fuse = moe is not None
    assert s % tm == 0 and nout % tn == 0 and rope_cols % tn == 0 and norm_cols % tn == 0 and q_cols % tn == 0
    im = lambda f: (lambda bi, si, *_: f(bi, si))
    in_specs = [pl.BlockSpec((1, tm, d), im(lambda bi, si: (bi, si, 0))),
                pl.BlockSpec((1, 6, d), im(lambda bi, si: (bi, 0, 0))),
                pl.BlockSpec((1, 6, d), im(lambda bi, si: (CTX_MOD_ROW, 0, 0))),
                pl.BlockSpec((1, d), im(lambda bi, si: (0, 0))),
                pl.BlockSpec((d, nout), im(lambda bi, si: (0, 0)))]
    args = [x, mods, mods, g.reshape(1, d), w]
    if rope_cols:
        in_specs += [pl.BlockSpec((tm, LANES), im(lambda bi, si: (si, 0)))] * 2
        args += list(rope)
    if norm_cols:
        seg = np.kron(np.eye(tn // HEAD_DIM), np.ones((HEAD_DIM, HEAD_DIM))).astype(np.float32)
        in_specs += [pl.BlockSpec((1, norm_cols), im(lambda bi, si: (0, 0))),
                     pl.BlockSpec((tn, tn), im(lambda bi, si: (0, 0)))]
        args += [head_gain, jnp.asarray(seg, BF16)]
    out_specs = [pl.BlockSpec((1, tm, nout), im(lambda bi, si: (bi, si, 0)))]
    out_shape = [jax.ShapeDtypeStruct((b, s, nout), BF16)]
    scratch = []
    n_prefetch = 0
    if fuse:
        y_sorted, pos, prev_mods = moe
        in_specs += [pl.BlockSpec(memory_space=pl.ANY),
                     pl.BlockSpec((1, 6, d), im(lambda bi, si: (bi, 0, 0))),
                     pl.BlockSpec((1, 6, d), im(lambda bi, si: (CTX_MOD_ROW, 0, 0)))]
        args = [pos] + args + [y_sorted, prev_mods, prev_mods]
        out_specs.append(pl.BlockSpec((1, tm, d), im(lambda bi, si: (bi, si, 0))))
        out_shape.append(jax.ShapeDtypeStruct((b, s, d), F32))
        scratch = [pltpu.VMEM((2, tm, d), F32), pltpu.SemaphoreType.DMA((2,))]
        n_prefetch = 1
    kern = functools.partial(_proj_kernel, tm=tm, tn=tn, nout=nout, ctx_rows=ctx_rows, q_cols=q_cols,
                             rope_cols=rope_cols, norm_cols=norm_cols, fuse_moe=fuse, n_s=n_s, n_tiles=b * n_s)
    res = pl.pallas_call(
        kern,
        grid_spec=pltpu.PrefetchScalarGridSpec(num_scalar_prefetch=n_prefetch, grid=(b, n_s), in_specs=in_specs,
                                               out_specs=out_specs, scratch_shapes=scratch),
        out_shape=out_shape, compiler_params=_cparams(2), name="proj",
    )(*args)
    return (res[0], res[1]) if fuse else (res[0], x)


def _lane_halves():
    lane = lax.broadcasted_iota(jnp.int32, (1, LANES), 1)
    return lane < HEAD_DIM


def _head_half(x, lo, half, fill=0.0):
    f = jnp.full_like(x, fill)
    return jnp.where(lo, x, f) if half == 0 else jnp.where(lo, f, x)


def _exp2_parts(s_list):
    m = functools.reduce(jnp.maximum, [jnp.max(s, axis=-1, keepdims=True) for s in s_list])
    return [jnp.exp2(s - m) for s in s_list]


def _pv_sum(o, half):
    l = o[:, HEAD_DIM:HEAD_DIM + 1] if half == 0 else o[:, 0:1]
    return o * (1.0 / l)


def _diff_attn_kernel(lam_ref, q_ref, k_ref, v_ref, g_ref, o_ref, *, lam_init):
    lv = lam_ref[...]
    lam = (jnp.exp(jnp.sum(lv[0:1] * lv[1:2], axis=-1, keepdims=True))
           - jnp.exp(jnp.sum(lv[2:3] * lv[3:4], axis=-1, keepdims=True)) + lam_init)
    lo = _lane_halves()

    def attend(nk):
        q = q_ref[0]
        k = k_ref[0, :nk]
        (e1,) = _exp2_parts([_dot_nt(_head_half(q, lo, 0), k)])
        (e2,) = _exp2_parts([_dot_nt(_head_half(q, lo, 1), k)])
        l1 = jnp.sum(e1, axis=-1, keepdims=True)
        l2 = jnp.sum(e2, axis=-1, keepdims=True)
        a = e1 - e2 * (lam * l1 / l2)
        o = _dot(a.astype(BF16), v_ref[0, :nk]) * (1.0 / l1)
        o_ref[0] = (_rms(o, g_ref[...]) * (1.0 - lam_init)).astype(BF16)

    qi = pl.program_id(2)

    @pl.when(qi == 0)
    def _():
        attend(CTX_LEN)

    @pl.when(qi > 0)
    def _():
        attend(S_ALL)


def _diff_attn(qkv, lam_vecs, subln_g, lam_init):
    b = qkv.shape[0]
    tq = CTX_LEN
    h = DIFF_HEADS
    kern = functools.partial(_diff_attn_kernel, lam_init=lam_init)
    return pl.pallas_call(
        kern, grid=(b, h, S_ALL // tq),
        in_specs=[pl.BlockSpec((4, HEAD_DIM), lambda bi, hi, qi: (0, 0)),
                  pl.BlockSpec((1, tq, LANES), lambda bi, hi, qi: (bi, qi, hi)),
                  pl.BlockSpec((1, S_ALL, LANES), lambda bi, hi, qi: (bi, 0, h + hi)),
                  pl.BlockSpec((1, S_ALL, LANES), lambda bi, hi, qi: (bi, 0, 2 * h + hi)),
                  pl.BlockSpec((1, LANES), lambda bi, hi, qi: (0, 0))],
        out_specs=pl.BlockSpec((1, tq, LANES), lambda bi, hi, qi: (bi, qi, hi)),
        out_shape=jax.ShapeDtypeStruct((b, S_ALL, D_MODEL), BF16),
        compiler_params=_cparams(3), name="diff_attn",
    )(lam_vecs, qkv, qkv, qkv, subln_g.reshape(1, LANES))


NA_ROWS = SEQ // GRID_W
NA_QROWS = 4
NA_QBLK = NA_QROWS * GRID_W
NA_BROWS = NA_QROWS + NA_WR
NA_BKEYS = NA_BROWS * GRID_W
NA_NBLK = NA_ROWS // NA_QROWS
NA_VARIANTS = 3
NA_BASE_ROWS = 2 * NA_WR


def _na_row_geometry():
    row_idx = np.zeros((NA_VARIANTS, NA_QROWS, NA_BROWS), np.int32)
    row_valid = np.zeros((NA_VARIANTS, NA_QROWS, NA_BROWS), bool)
    for var, rb in enumerate((0, 1, NA_NBLK - 1)):
        s0 = np.clip(NA_QROWS * rb - NA_WR // 2, 0, NA_ROWS - NA_BROWS)
        for a in range(NA_QROWS):
            r = NA_QROWS * rb + a
            rs = np.clip(r - NA_WR // 2, 0, NA_ROWS - NA_WR)
            for bb in range(NA_BROWS):
                kr = s0 + bb
                row_valid[var, a, bb] = rs <= kr < rs + NA_WR
                row_idx[var, a, bb] = kr - r + NA_WR - 1
    return row_idx, row_valid


def _na_build_bias(base_ref, bias_ref):
    row_idx, row_valid = _na_row_geometry()
    qc = lax.broadcasted_iota(jnp.int32, (GRID_W, LANES), 0)
    lane = lax.broadcasted_iota(jnp.int32, (GRID_W, LANES), 1)
    kc = lane % GRID_W
    win0 = jnp.clip(qc - NA_WC // 2, 0, GRID_W - NA_WC)
    col_valid = (kc >= win0) & (kc < win0 + NA_WC)
    low = lane < GRID_W
    for h in range(2):
        for var in range(NA_VARIANTS):
            for a in range(NA_QROWS):
                for bp in range(NA_BROWS // 2):
                    v0, v1 = row_valid[var, a, 2 * bp], row_valid[var, a, 2 * bp + 1]
                    rows = slice(a * GRID_W, (a + 1) * GRID_W)
                    cols = slice(bp * LANES, (bp + 1) * LANES)
                    if not (v0 or v1):
                        bias_ref[h, var, rows, cols] = jnp.full((GRID_W, LANES), NEG_INF, F32)
                        continue
                    k = int(row_idx[var, a, 2 * bp + 1])
                    slab = jnp.broadcast_to(base_ref[h, k:k + 1, :], (GRID_W, LANES))
                    slab = pltpu.roll(slab, LANES - (NA_WC - 1), 1, stride=1, stride_axis=0)
                    if v0 and v1:
                        ok = col_valid
                    elif v0:
                        ok = jnp.logical_and(col_valid, low)
                    else:
                        ok = jnp.logical_and(col_valid, jnp.logical_not(low))
                    bias_ref[h, var, rows, cols] = jnp.where(ok, slab, NEG_INF)


def _na_attn_kernel(q_ref, k_ref, v_ref, base_ref, o_ref, bias_ref):
    @pl.when(pl.program_id(1) == 0)
    def _():
        _na_build_bias(base_ref, bias_ref)

    lo = _lane_halves()
    kc = k_ref[0, :CTX_LEN]
    vc = v_ref[0, :CTX_LEN]
    vc_aug = [_head_half(vc, lo, h, 1.0) for h in range(2)]

    q = q_ref[0, :CTX_LEN]
    outs = []
    for h in range(2):
        (e,) = _exp2_parts([_dot_nt(_head_half(q, lo, h), kc)])
        outs.append(_pv_sum(_dot(e.astype(BF16), vc_aug[h]), h))
    o_ref[0, :CTX_LEN] = jnp.where(lo, outs[0], outs[1]).astype(BF16)

    def block(rb, carry):
        s0 = jnp.clip(NA_QROWS * rb - NA_WR // 2, 0, NA_ROWS - NA_BROWS)
        var = jnp.where(rb == 0, 0, jnp.where(rb == NA_NBLK - 1, 2, 1))
        q0 = pl.multiple_of(CTX_LEN + rb * NA_QBLK, NA_QBLK)
        b0 = pl.multiple_of(CTX_LEN + s0 * GRID_W, GRID_W)
        q = q_ref[0, pl.ds(q0, NA_QBLK)]
        kb = k_ref[0, pl.ds(b0, NA_BKEYS)]
        vb = v_ref[0, pl.ds(b0, NA_BKEYS)]
        outs = []
        for h in range(2):
            qh = _head_half(q, lo, h)
            e_loc, e_ctx = _exp2_parts([_dot_nt(qh, kb) + bias_ref[h, var], _dot_nt(qh, kc)])
            o = _dot(e_loc.astype(BF16), _head_half(vb, lo, h, 1.0)) + _dot(e_ctx.astype(BF16), vc_aug[h])
            outs.append(_pv_sum(o, h))
        o_ref[0, pl.ds(q0, NA_QBLK)] = jnp.where(lo, outs[0], outs[1]).astype(BF16)
        return carry

    lax.fori_loop(0, NA_NBLK, block, 0)


def _na_base_rows(rpb):
    n_c = 2 * NA_WC - 1
    r = rpb.astype(F32) * LOG2E
    zrow = jnp.zeros((NA_HEADS, 1, n_c), F32)
    low = jnp.concatenate([zrow, r], axis=1)
    high = jnp.concatenate([r, zrow], axis=1)
    pad = jnp.zeros((NA_HEADS, NA_BASE_ROWS, GRID_W - n_c), F32)
    return jnp.concatenate([low, pad, high, pad], axis=2)


def _na_attn(qkv, base):
    b = qkv.shape[0]
    hp = NA_HEADS // 2
    return pl.pallas_call(
        _na_attn_kernel, grid=(hp, b),
        in_specs=[pl.BlockSpec((1, S_ALL, LANES), lambda hi, bi: (bi, 0, hi)),
                  pl.BlockSpec((1, S_ALL, LANES), lambda hi, bi: (bi, 0, hp + hi)),
                  pl.BlockSpec((1, S_ALL, LANES), lambda hi, bi: (bi, 0, 2 * hp + hi)),
                  pl.BlockSpec((2, NA_BASE_ROWS, LANES), lambda hi, bi: (hi, 0, 0))],
        out_specs=pl.BlockSpec((1, S_ALL, LANES), lambda hi, bi: (bi, 0, hi)),
        out_shape=jax.ShapeDtypeStruct((b, S_ALL, D_MODEL), BF16),
        scratch_shapes=[pltpu.VMEM((2, NA_VARIANTS, NA_QBLK, NA_BKEYS), F32)],
        compiler_params=_cparams(2), name="na_attn",
    )(qkv, qkv, qkv, base)


GQA_GROUP = GQA_Q_HEADS // GQA_KV_HEADS


def _gqa_attn_kernel(q_ref, k_ref, v_ref, o_ref, *, tq):
    lo = _lane_halves()
    k = k_ref[0]
    v = v_ref[0]
    outs = [[None, None] for _ in range(GQA_GROUP)]
    for j in range(2):
        qs = []
        for g in range(GQA_GROUP):
            slab, half = divmod(j * GQA_GROUP + g, 2)
            q = q_ref[0, :, slab * LANES:(slab + 1) * LANES]
            if half != j:
                q = jnp.concatenate([q[:, HEAD_DIM:], q[:, :HEAD_DIM]], axis=1)
            qs.append(_head_half(q, lo, j))
        (e,) = _exp2_parts([_dot_nt(jnp.concatenate(qs, axis=0), k)])
        o = _pv_sum(_dot(e.astype(BF16), _head_half(v, lo, j, 1.0)), j)
        for g in range(GQA_GROUP):
            slab, half = divmod(j * GQA_GROUP + g, 2)
            og = o[g * tq:(g + 1) * tq]
            if half != j:
                og = pltpu.roll(og, HEAD_DIM, 1)
            outs[slab][half] = og
    for slab in range(GQA_GROUP):
        o_ref[0, :, slab * LANES:(slab + 1) * LANES] = jnp.where(lo, outs[slab][0], outs[slab][1]).astype(BF16)


def _gqa_attn(qkv):
    b = qkv.shape[0]
    tq = 256
    qw = GQA_GROUP * 2 * HEAD_DIM
    n_pairs = GQA_KV_HEADS // 2
    kv0 = GQA_Q_HEADS * HEAD_DIM // LANES
    kern = functools.partial(_gqa_attn_kernel, tq=tq)
    return pl.pallas_call(
        kern, grid=(b, n_pairs, SEQ // tq),
        in_specs=[pl.BlockSpec((1, tq, qw), lambda bi, pi, qi: (bi, qi + CTX_LEN // tq, pi)),
                  pl.BlockSpec((1, S_ALL, LANES), lambda bi, pi, qi: (bi, 0, kv0 + pi)),
                  pl.BlockSpec((1, S_ALL, LANES), lambda bi, pi, qi: (bi, 0, kv0 + n_pairs + pi))],
        out_specs=pl.BlockSpec((1, tq, qw), lambda bi, pi, qi: (bi, qi, pi)),
        out_shape=jax.ShapeDtypeStruct((b, SEQ, D_MODEL), BF16),
        compiler_params=_cparams(3), name="gqa_attn",
    )(qkv, qkv, qkv)


def _conv_kernel(gb_ref, gc_ref, u_ref, w_ref, o_ref):
    p = gc_ref[0].astype(F32) * u_ref[0].astype(F32)
    n = p.shape[0]
    rows = lax.broadcasted_iota(jnp.int32, (n, 1), 0)
    prev = jnp.where(rows == 0, 0.0, pltpu.roll(p, 1, 0))
    nxt = jnp.where(rows == n - 1, 0.0, pltpu.roll(p, n - 1, 0))
    w = w_ref[...]
    z = prev * w[0:1] + p * w[1:2] + nxt * w[2:3]
    o_ref[0] = (gb_ref[0].astype(F32) * z).astype(BF16)


def _conv_mix(proj, w_conv):
    b, n, _ = proj.shape
    nb = D_MODEL // LANES
    spec = lambda off: pl.BlockSpec((1, n, LANES), lambda bi, ci: (bi, 0, off + ci))
    return pl.pallas_call(
        _conv_kernel, grid=(b, nb),
        in_specs=[spec(0), spec(nb), spec(2 * nb), pl.BlockSpec((3, LANES), lambda bi, ci: (0, ci))],
        out_specs=pl.BlockSpec((1, n, LANES), lambda bi, ci: (bi, 0, ci)),
        out_shape=jax.ShapeDtypeStruct((b, n, D_MODEL), BF16),
        compiler_params=_cparams(2), name="conv_mix",
    )(proj, proj, proj, w_conv)


def _route(logits_t, bias):
    scores = _sigmoid(logits_t)
    sel = scores + bias
    nt = sel.shape[1]
    idx = lax.broadcasted_iota(jnp.int32, (N_EXPERTS, nt), 0).astype(F32)
    best = gidx = None
    for g in range(N_GROUPS):
        r = [sel[g * GROUP_SIZE + i:g * GROUP_SIZE + i + 1] for i in range(GROUP_SIZE)]
        pair_sums = [r[i] + r[j] for i in range(GROUP_SIZE) for j in range(i + 1, GROUP_SIZE)]
        gs = functools.reduce(jnp.maximum, pair_sums)
        if g == 0:
            best, gidx = gs, jnp.zeros_like(gs)
        else:
            upd = gs > best
            best = jnp.where(upd, gs, best)
            gidx = jnp.where(upd, float(g), gidx)
    in_group = jnp.floor(idx * (1.0 / GROUP_SIZE)) == gidx
    masked = jnp.where(in_group, sel, NEG_INF)
    m1 = jnp.max(masked, axis=0, keepdims=True)
    i1 = jnp.min(jnp.where(masked == m1, idx, float(N_EXPERTS)), axis=0, keepdims=True)
    masked2 = jnp.where(idx == i1, -jnp.inf, masked)
    m2 = jnp.max(masked2, axis=0, keepdims=True)
    i2 = jnp.min(jnp.where(masked2 == m2, idx, float(N_EXPERTS)), axis=0, keepdims=True)
    w1 = jnp.sum(jnp.where(idx == i1, scores, 0.0), axis=0, keepdims=True)
    w2 = jnp.sum(jnp.where(idx == i2, scores, 0.0), axis=0, keepdims=True)
    inv = 1.0 / (w1 + w2)
    gates = jnp.where(idx == i1, w1 * inv, 0.0) + jnp.where(idx == i2, w2 * inv, 0.0)
    rows = []
    for i in range(GROUP_SIZE):
        rows.append(functools.reduce(
            lambda a, b: a + b,
            [jnp.where(gidx == float(g), gates[g * GROUP_SIZE + i:g * GROUP_SIZE + i + 1], 0.0) for g in range(N_GROUPS)]))
    rows.append(gidx)
    rows += [jnp.zeros_like(gidx)] * (ROUTE_ROWS - len(rows))
    return jnp.concatenate(rows, axis=0)


def _out_kernel(o_ref, x_ref, mb_ref, mc_ref, g_ref, wo_ref, rwh_ref, rwl_ref, rb_ref,
                xn_ref, h_ref, rt_ref, *, tm, ctx_rows):
    row0 = pl.program_id(1) * tm
    mod = lambda k: _row_mod(mb_ref, mc_ref, k, row0, tm, ctx_rows)
    x = x_ref[0] + mod(2) * _dot(o_ref[0], wo_ref[...])
    xn_ref[0] = x
    h = _norm_mod(x, g_ref[...], mod(3), mod(4))
    hb = h.astype(BF16)
    hl = (h - hb.astype(F32)).astype(BF16)
    logits_t = _dot_nt(rwh_ref[...], hb) + _dot_nt(rwh_ref[...], hl) + _dot_nt(rwl_ref[...], hb)
    route = _route(logits_t, rb_ref[...])
    rt_ref[0] = route
    h_ref[0, :, :D_MODEL] = h
    h_ref[0, :, D_MODEL:] = jnp.concatenate([route, jnp.zeros((LANES - ROUTE_ROWS, tm), F32)], axis=0).T


def _out_proj(o, x, mods, g, w_o, rw_hi, rw_lo, rbias, *, tm, ctx_rows, x_row_off=0):
    b, s, d = o.shape
    assert s % tm == 0 and x_row_off % tm == 0
    xo = x_row_off // tm
    kern = functools.partial(_out_kernel, tm=tm, ctx_rows=ctx_rows)
    full = lambda shape: pl.BlockSpec(shape, lambda bi, si: (0,) * len(shape))
    return pl.pallas_call(
        kern, grid=(b, s // tm),
        in_specs=[pl.BlockSpec((1, tm, d), lambda bi, si: (bi, si, 0)),
                  pl.BlockSpec((1, tm, d), lambda bi, si: (bi, si + xo, 0)),
                  pl.BlockSpec((1, 6, d), lambda bi, si: (bi, 0, 0)),
                  pl.BlockSpec((1, 6, d), lambda bi, si: (CTX_MOD_ROW, 0, 0)),
                  full((1, d)), full((d, d)), full((N_EXPERTS, d)), full((N_EXPERTS, d)), full((N_EXPERTS, 1))],
        out_specs=[pl.BlockSpec((1, tm, d), lambda bi, si: (bi, si, 0)),
                   pl.BlockSpec((1, tm, H_EXT), lambda bi, si: (bi, si, 0)),
                   pl.BlockSpec((1, ROUTE_ROWS, tm), lambda bi, si: (bi, 0, si))],
        out_shape=[jax.ShapeDtypeStruct((b, s, d), F32),
                   jax.ShapeDtypeStruct((b, s, H_EXT), F32),
                   jax.ShapeDtypeStruct((b, ROUTE_ROWS, s), F32)],
        compiler_params=_cparams(2), name="out_proj",
    )(o, x, mods, mods, g.reshape(1, d), w_o, rw_hi, rw_lo, rbias)


def _moe_kernel(rank_ref, tbase_ref, tlast_ref, tgroup_ref, nact_ref, h_hbm, wg_ref, wu_ref, wd_ref, o_ref,
                perm, buf, sem, *, tm, n_tok):
    i = pl.program_id(0)
    n_act = nact_ref[0]

    def row_of(tile):
        base, last = tbase_ref[tile], tlast_ref[tile]
        return lambda j: perm[jnp.minimum(base + j, last)]

    @pl.when(i == 0)
    def _():
        def inv(t, carry):
            perm[rank_ref[t]] = t
            return carry

        lax.fori_loop(0, n_tok, inv, 0, unroll=8)
        _start_rows_loop(row_of(0), h_hbm, buf, sem, 0, tm)

    @pl.when(i < n_act)
    def _():
        slot = i % 2
        _wait_rows(h_hbm, buf, sem, slot, tm)
        prefetch = _RowPrefetch(row_of(jnp.minimum(i + 1, n_act - 1)), h_hbm, buf, sem, 1 - slot, tm, 3 * GROUP_SIZE)
        hb = buf[slot, :, :D_MODEL].astype(BF16)
        gates = buf[slot, :, D_MODEL:]
        acc = jnp.zeros((tm, D_MODEL), F32)
        for j in range(GROUP_SIZE):
            g = _dot(hb, wg_ref[j])
            prefetch.issue_piece()
            u = _dot(hb, wu_ref[j])
            prefetch.issue_piece()
            a = (g * _sigmoid(g)) * u * gates[:, j:j + 1]
            acc = acc + _dot(a.astype(BF16), wd_ref[j])
            prefetch.issue_piece()
        o_ref[...] = acc

        @pl.when(i == n_act - 1)
        def _():
            _wait_rows(h_hbm, buf, sem, 1 - slot, tm)

    @pl.when(i >= n_act)
    def _():
        o_ref[...] = jnp.zeros_like(o_ref)


def _moe(h_ext, plan, layer, wg, wu, wd):
    tm = MOE_TM
    rank, tile_base, tile_last, tile_group, n_active = plan
    n_tok = rank.shape[0]
    n_tiles = tile_group.shape[0]
    d = D_MODEL
    wspec = lambda shape: pl.BlockSpec((None,) + shape, lambda i, rk, tb, tl, tg, na: (layer, tg[i], 0, 0))
    kern = functools.partial(_moe_kernel, tm=tm, n_tok=n_tok)
    return pl.pallas_call(
        kern,
        grid_spec=pltpu.PrefetchScalarGridSpec(
            num_scalar_prefetch=5, grid=(n_tiles,),
            in_specs=[pl.BlockSpec(memory_space=pl.ANY),
                      wspec((GROUP_SIZE, d, EXPERT_FF)), wspec((GROUP_SIZE, d, EXPERT_FF)),
                      wspec((GROUP_SIZE, EXPERT_FF, d))],
            out_specs=pl.BlockSpec((tm, d), lambda i, *_: (i, 0)),
            scratch_shapes=[pltpu.SMEM((n_tok,), jnp.int32), pltpu.VMEM((2, tm, H_EXT), F32),
                            pltpu.SemaphoreType.DMA((2,))]),
        out_shape=jax.ShapeDtypeStruct((n_tiles * tm, d), F32),
        compiler_params=_cparams(1), name="moe",
    )(rank, tile_base, tile_last, tile_group, n_active, h_ext, wg, wu, wd)


def _moe_plan(route, tm):
    b, _, s = route.shape
    t = b * s
    n_tiles = t // tm + N_GROUPS
    gid = route[:, GROUP_SIZE, :].reshape(t).astype(jnp.int32)
    onehot = (gid[:, None] == jnp.arange(N_GROUPS, dtype=jnp.int32)[None, :]).astype(jnp.int32)
    csum = jnp.cumsum(onehot, axis=0)
    counts = csum[-1]
    in_group_rank = jnp.sum((csum - onehot) * onehot, axis=1)
    group_start = jnp.cumsum(counts) - counts
    tiles = (counts + tm - 1) // tm
    tile_end = jnp.cumsum(tiles)
    tile_start = tile_end - tiles
    pos = jnp.sum(onehot * tile_start[None, :], axis=1) * tm + in_group_rank
    rank = jnp.sum(onehot * group_start[None, :], axis=1) + in_group_rank
    tile_ids = jnp.arange(n_tiles, dtype=jnp.int32)
    tile_group = jnp.minimum(jnp.sum((tile_ids[:, None] >= tile_end[None, :]).astype(jnp.int32), axis=1), N_GROUPS - 1)
    tile_base = group_start[tile_group] + (tile_ids - tile_start[tile_group]) * tm
    tile_last = jnp.maximum(group_start[tile_group] + counts[tile_group] - 1, 0)
    tile_base = jnp.minimum(tile_base, tile_last)
    i32 = lambda a: a.astype(jnp.int32)
    return i32(pos), (i32(rank), i32(tile_base), i32(tile_last), i32(tile_group), i32(tile_end[-1:]))


def _combine_kernel(pos_ref, x_ref, y_hbm, mb_ref, mc_ref, fg_ref, o_ref, buf, sem, *, tm, ctx_rows, tiles_per_sample, n_tiles):
    i = pl.program_id(0)
    slot = i % 2

    @pl.when(i == 0)
    def _():
        _start_rows_loop(lambda j: pos_ref[j], y_hbm, buf, sem, 0, tm)

    @pl.when(i + 1 < n_tiles)
    def _():
        nxt = (i + 1) * tm
        _start_rows_loop(lambda j: pos_ref[nxt + j], y_hbm, buf, sem, 1 - slot, tm)

    _wait_rows(y_hbm, buf, sem, slot, tm)
    row0 = (i % tiles_per_sample) * tm
    x = x_ref[...] + _row_mod(mb_ref, mc_ref, 5, row0, tm, ctx_rows) * buf[slot]
    o_ref[...] = _rms(x, fg_ref[...])


def _combine_final(x, y_sorted, pos, mods, final_g, *, tm, ctx_rows, s):
    t, d = x.shape
    tps = s // tm
    n_tiles = t // tm
    kern = functools.partial(_combine_kernel, tm=tm, ctx_rows=ctx_rows, tiles_per_sample=tps, n_tiles=n_tiles)
    return pl.pallas_call(
        kern,
        grid_spec=pltpu.PrefetchScalarGridSpec(
            num_scalar_prefetch=1, grid=(n_tiles,),
            in_specs=[pl.BlockSpec((tm, d), lambda i, pos: (i, 0)),
                      pl.BlockSpec(memory_space=pl.ANY),
                      pl.BlockSpec((1, 6, d), lambda i, pos: (i // tps, 0, 0)),
                      pl.BlockSpec((1, 6, d), lambda i, pos: (CTX_MOD_ROW, 0, 0)),
                      pl.BlockSpec((1, d), lambda i, pos: (0, 0))],
            out_specs=pl.BlockSpec((tm, d), lambda i, pos: (i, 0)),
            scratch_shapes=[pltpu.VMEM((2, tm, d), F32), pltpu.SemaphoreType.DMA((2,))]),
        out_shape=jax.ShapeDtypeStruct((t, d), F32),
        compiler_params=_cparams(1), name="combine",
    )(pos, x, y_sorted, mods, mods, final_g.reshape(1, d))


def _rope_tables():
    t = jnp.arange(SEQ)
    row = (t // GRID_W).astype(F32)
    col = (t % GRID_W).astype(F32)
    quarter = HEAD_DIM // 4
    inv_freq = ROPE_THETA ** (-jnp.arange(quarter, dtype=F32) / quarter)
    ang = jnp.concatenate([row[:, None] * inv_freq, col[:, None] * inv_freq], axis=-1)
    cos, sin = jnp.cos(ang), jnp.sin(ang)
    reps = LANES // HEAD_DIM
    cos_l = jnp.tile(jnp.concatenate([cos, cos], axis=-1), (1, reps))
    sin_l = jnp.tile(jnp.concatenate([-sin, sin], axis=-1), (1, reps))
    cos_all = jnp.concatenate([jnp.ones((CTX_LEN, LANES), F32), cos_l], axis=0)
    sin_all = jnp.concatenate([jnp.zeros((CTX_LEN, LANES), F32), sin_l], axis=0)
    return cos_all, sin_all


def kernel(x, c, ctx, c_ctx, ada_w, ada_b, norm1_g, norm2_g, final_norm_g, diff_w_qkv, diff_w_o, diff_lambda_q1, diff_lambda_k1, diff_lambda_q2, diff_lambda_k2, diff_subln_g, na_w_qkv, na_w_o, na_rpb, gqa_w_qkv, gqa_w_o, gqa_q_norm_g, gqa_k_norm_g, conv_w_in, conv_w, conv_w_out, router_w, router_bias, moe_w_gate, moe_w_up, moe_w_down):
    assert DEPTH % N_MIXERS == 0
    b = x.shape[0]
    cc = jnp.zeros((MOD_ROWS, D_MODEL), F32).at[:b].set(c).at[CTX_MOD_ROW].set(c_ctx)
    mods_all = _ada(cc, ada_w, ada_b).reshape(DEPTH, MOD_ROWS, 6, D_MODEL)
    rope = _rope_tables()
    rw_t = router_w.T
    rw_hi = rw_t.astype(BF16)
    rw_lo = (rw_t - rw_hi.astype(F32)).astype(BF16)
    rbias = router_bias.reshape(N_EXPERTS, 1).astype(F32)
    wg_all, wu_all, wd_all = moe_w_gate.astype(BF16), moe_w_up.astype(BF16), moe_w_down.astype(BF16)

    xs = jnp.concatenate([ctx, x], axis=1)
    moe_pending = None
    for i in range(DEPTH):
        m, j = i % N_MIXERS, i // N_MIXERS
        mods = mods_all[i]
        has_ctx = xs.shape[1] == S_ALL
        ctx_rows = CTX_LEN if has_ctx else 0
        ctx_advances = any(k % N_MIXERS != 3 for k in range(i + 1, DEPTH))
        common = dict(tm=768 if has_ctx else 512, ctx_rows=ctx_rows, moe=moe_pending)
        if m == 0:
            lam_init = 0.8 - 0.6 * math.exp(-0.3 * i)
            qkv, xs = _proj(xs, mods, norm1_g[i], diff_w_qkv[j].astype(BF16), q_cols=D_MODEL,
                            rope=rope, rope_cols=2 * D_MODEL, **common)
            lam_vecs = jnp.stack([diff_lambda_q1[j], diff_lambda_k1[j], diff_lambda_q2[j], diff_lambda_k2[j]])
            o = _diff_attn(qkv, lam_vecs, diff_subln_g[j], lam_init)
            w_o = diff_w_o[j]
        elif m == 1:
            qkv, xs = _proj(xs, mods, norm1_g[i], na_w_qkv[j].astype(BF16), q_cols=D_MODEL, **common)
            o = _na_attn(qkv, _na_base_rows(na_rpb[j]))
            w_o = na_w_o[j]
        elif m == 2:
            qk_cols = (GQA_Q_HEADS + GQA_KV_HEADS) * HEAD_DIM
            gain = jnp.concatenate([jnp.tile(gqa_q_norm_g[j] * QK_SCALE, GQA_Q_HEADS),
                                    jnp.tile(gqa_k_norm_g[j], GQA_KV_HEADS)]).reshape(1, qk_cols)
            qkv, xs = _proj(xs, mods, norm1_g[i], gqa_w_qkv[j].astype(BF16),
                            rope=rope, rope_cols=qk_cols, head_gain=gain, norm_cols=qk_cols, **common)
            o = _gqa_attn(qkv)
            w_o = gqa_w_o[j]
        else:
            proj, xs = _proj(xs, mods, norm1_g[i], conv_w_in[j].astype(BF16), **common)
            o = _conv_mix(proj, conv_w[j])
            w_o = conv_w_out[j]

        if has_ctx and not ctx_advances:
            tm_out, x_off, out_ctx_rows = CTX_LEN, CTX_LEN, 0
            if o.shape[1] == S_ALL:
                o = o[:, CTX_LEN:]
        elif has_ctx:
            tm_out, x_off, out_ctx_rows = 768, 0, CTX_LEN
        else:
            tm_out, x_off, out_ctx_rows = 1024, 0, 0
        xs, h_ext, route = _out_proj(o, xs, mods, norm2_g[i], w_o.astype(BF16), rw_hi, rw_lo, rbias,
                                     tm=tm_out, ctx_rows=out_ctx_rows, x_row_off=x_off)
        s = xs.shape[1]
        pos, plan = _moe_plan(route, MOE_TM)
        y_sorted = _moe(h_ext.reshape(b * s, H_EXT), plan, i, wg_all, wu_all, wd_all)
        moe_pending = (y_sorted, pos, mods)
    y_sorted, pos, mods = moe_pending
    s = xs.shape[1]
    return _combine_final(xs.reshape(b * s, D_MODEL), y_sorted, pos, mods, final_norm_g,
                          tm=1024, ctx_rows=0, s=s).reshape(b, s, D_MODEL)
```

```python
import functools
import math

import numpy as np
import jax
import jax.numpy as jnp
from jax import lax
from jax.experimental import pallas as pl
from jax.experimental.pallas import tpu as pltpu

D_MODEL = 1024
BATCH = 8
SEQ = 2048
DEPTH = 4
GRID_W = 64
CTX_LEN = 256
S_ALL = CTX_LEN + SEQ
N_MIXERS = 4
HEAD_DIM = 64
ATTN_SCALE = HEAD_DIM ** -0.5
LOG2E = math.log2(math.e)
QK_SCALE = ATTN_SCALE * LOG2E
DIFF_HEADS = D_MODEL // (2 * HEAD_DIM)
NA_HEADS = D_MODEL // HEAD_DIM
NA_WR = 8
NA_WC = 16
GQA_Q_HEADS = D_MODEL // HEAD_DIM
GQA_KV_HEADS = 4
ROPE_THETA = 10000.0
N_EXPERTS = 16
N_GROUPS = 4
GROUP_SIZE = N_EXPERTS // N_GROUPS
EXPERT_FF = 512
NORM_EPS = 1e-6
NEG_INF = -1e30

LANES = 128
SUBLANES = 8
MOD_ROWS = 16
CTX_MOD_ROW = BATCH
H_EXT = D_MODEL + LANES
ROUTE_ROWS = SUBLANES
MOE_TM = 512
MAX_DMA_BURST = 64
ATTN_KCHUNK = 768
DIFF_TQ = 512
F32 = jnp.float32
BF16 = jnp.bfloat16
VMEM_LIMIT = 56 * 1024 * 1024

_NT = (((1,), (1,)), ((), ()))


def _cparams(n_axes):
    return pltpu.CompilerParams(dimension_semantics=("arbitrary",) * n_axes,
                                vmem_limit_bytes=VMEM_LIMIT)


def _dot(a, b):
    return jnp.dot(a, b, preferred_element_type=F32)


def _dot_nt(a, b):
    return lax.dot_general(a, b, _NT, preferred_element_type=F32)


def _sigmoid(x):
    return 1.0 / (1.0 + jnp.exp(-x))


def _row_mod(mb_ref, mc_ref, k, row0, tm, ctx_rows):
    mb = mb_ref[0, k:k + 1, :]
    if ctx_rows == 0:
        return mb
    rows = row0 + lax.broadcasted_iota(jnp.int32, (tm, 1), 0)
    return jnp.where(rows < ctx_rows, mc_ref[0, k:k + 1, :], mb)


def _rms(x, g):
    return x * lax.rsqrt(jnp.mean(x * x, axis=-1, keepdims=True) + NORM_EPS) * g


def _norm_mod(x, g, shift, scale):
    return _rms(x, g) * (1.0 + scale) + shift


def _row_copy(src_hbm, buf, sem, slot, src_row, dst_row):
    return pltpu.make_async_copy(src_hbm.at[pl.ds(src_row, 1)], buf.at[slot, pl.ds(dst_row, 1)], sem.at[slot])


def _wait_rows(src_hbm, buf, sem, slot, n):
    pltpu.make_async_copy(src_hbm.at[pl.ds(0, n)], buf.at[slot], sem.at[slot]).wait()


def _start_rows_loop(row_of, src_hbm, buf, sem, slot, n):
    def body(j, carry):
        _row_copy(src_hbm, buf, sem, slot, row_of(j), j).start()
        return carry

    lax.fori_loop(0, n, body, 0, unroll=8)


def _split_rows(n, parts):
    edges = [round(k * n / parts) for k in range(parts + 1)]
    return list(zip(edges[:-1], edges[1:]))


class _RowPrefetch:
    def __init__(self, row_of, src_hbm, buf, sem, slot, n, parts):
        self.args = (row_of, src_hbm, buf, sem, slot)
        self.pieces = _split_rows(n, parts)
        self.k = 0

    def issue_piece(self):
        row_of, src_hbm, buf, sem, slot = self.args
        lo, hi = self.pieces[self.k]
        self.k += 1
        for j in range(lo, hi):
            _row_copy(src_hbm, buf, sem, slot, row_of(j), j).start()


def _ada_kernel(cc_ref, w_ref, b_ref, o_ref):
    cc = cc_ref[...]
    s = (cc * _sigmoid(cc)).astype(BF16)
    o_ref[0] = _dot(s, w_ref[0].astype(BF16)) + b_ref[0]


def _ada(cc, ada_w, ada_b):
    tn = 1536
    n = 6 * D_MODEL
    return pl.pallas_call(
        _ada_kernel,
        grid=(DEPTH, n // tn),
        in_specs=[pl.BlockSpec((MOD_ROWS, D_MODEL), lambda l, j: (0, 0)),
                  pl.BlockSpec((1, D_MODEL, tn), lambda l, j: (l, 0, j)),
                  pl.BlockSpec((1, 1, tn), lambda l, j: (l, 0, j))],
        out_specs=pl.BlockSpec((1, MOD_ROWS, tn), lambda l, j: (l, 0, j)),
        out_shape=jax.ShapeDtypeStruct((DEPTH, MOD_ROWS, n), F32),
        compiler_params=_cparams(2), name="ada",
    )(cc, ada_w, ada_b.reshape(DEPTH, 1, n))


def _rope(x, cos, sin_signed):
    lane = lax.broadcasted_iota(jnp.int32, (1, LANES), 1)
    first_half = (lane % HEAD_DIM) < (HEAD_DIM // 2)
    partner = jnp.where(first_half, pltpu.roll(x, LANES - HEAD_DIM // 2, 1), pltpu.roll(x, HEAD_DIM // 2, 1))
    return x * cos + partner * sin_signed


def _proj_kernel(*refs, tm, tn, nout, ctx_rows, q_cols, rope_cols, norm_cols, fuse_moe, n_s, n_tiles):
    refs = list(refs)
    pos_ref = refs.pop(0) if fuse_moe else None
    x_ref, mb_ref, mc_ref, g_ref, w_ref = refs[:5]
    rest = refs[5:]
    cos_ref = sin_ref = hg_ref = seg_ref = y_hbm = pmb_ref = pmc_ref = None
    if rope_cols:
        cos_ref, sin_ref = rest[:2]
        rest = rest[2:]
    if norm_cols:
        hg_ref, seg_ref = rest[:2]
        rest = rest[2:]
    if fuse_moe:
        y_hbm, pmb_ref, pmc_ref = rest[:3]
        o_ref, xn_ref, buf, sem = rest[3:]
    else:
        (o_ref,) = rest

    row0 = pl.program_id(1) * tm
    x = x_ref[0]
    prefetch = None
    n_chunks = nout // tn
    pieces_per_chunk = -(-tm // (n_chunks * MAX_DMA_BURST))
    if fuse_moe:
        lin = pl.program_id(0) * n_s + pl.program_id(1)
        slot = lin % 2

        @pl.when(lin == 0)
        def _():
            _start_rows_loop(lambda j: pos_ref[j], y_hbm, buf, sem, 0, tm)

        _wait_rows(y_hbm, buf, sem, slot, tm)
        x = x + _row_mod(pmb_ref, pmc_ref, 5, row0, tm, ctx_rows) * buf[slot]
        xn_ref[0] = x
        nxt = jnp.minimum(lin + 1, n_tiles - 1) * tm
        prefetch = _RowPrefetch(lambda j: pos_ref[nxt + j], y_hbm, buf, sem, 1 - slot, tm, n_chunks * pieces_per_chunk)

    shift = _row_mod(mb_ref, mc_ref, 0, row0, tm, ctx_rows)
    scale = _row_mod(mb_ref, mc_ref, 1, row0, tm, ctx_rows)
    hb = _norm_mod(x, g_ref[...], shift, scale).astype(BF16)
    for c in range(n_chunks):
        lo = c * tn
        acc = _dot(hb, w_ref[:, lo:lo + tn])
        if prefetch is not None:
            prefetch.issue_piece()
        if lo < q_cols:
            acc = acc * QK_SCALE
        if lo < norm_cols:
            sq = acc * acc
            sq_hi = sq.astype(BF16)
            sq_lo = (sq - sq_hi.astype(F32)).astype(BF16)
            ss = _dot(sq_hi, seg_ref[...]) + _dot(sq_lo, seg_ref[...])
            acc = acc * lax.rsqrt(ss * (1.0 / HEAD_DIM) + NORM_EPS) * hg_ref[:, lo:lo + tn]
        if lo < rope_cols:
            cos, sin = cos_ref[...], sin_ref[...]
            acc = jnp.concatenate(
                [_rope(acc[:, k * LANES:(k + 1) * LANES], cos, sin) for k in range(tn // LANES)], axis=1)
        o_ref[0, :, lo:lo + tn] = acc.astype(BF16)
        if prefetch is not None:
            for _ in range(pieces_per_chunk - 1):
                prefetch.issue_piece()

    if fuse_moe:
        @pl.when(lin == n_tiles - 1)
        def _():
            _wait_rows(y_hbm, buf, sem, 1 - slot, tm)


def _proj(x, mods, g, w, *, tm, ctx_rows, q_cols=0, rope=None, rope_cols=0, head_gain=None, norm_cols=0, moe=None):
    b, s, d = x.shape
    nout = w.shape[1]
    tn = 256
    n_s = s // tm
    fuse = moe is not None
    assert s % tm == 0 and nout % tn == 0 and rope_cols % tn == 0 and norm_cols % tn == 0 and q_cols % tn == 0
    im = lambda f: (lambda bi, si, *_: f(bi, si))
    in_specs = [pl.BlockSpec((1, tm, d), im(lambda bi, si: (bi, si, 0))),
                pl.BlockSpec((1, 6, d), im(lambda bi, si: (bi, 0, 0))),
                pl.BlockSpec((1, 6, d), im(lambda bi, si: (CTX_MOD_ROW, 0, 0))),
                pl.BlockSpec((1, d), im(lambda bi, si: (0, 0))),
                pl.BlockSpec((d, nout), im(lambda bi, si: (0, 0)))]
    args = [x, mods, mods, g.reshape(1, d), w]
    if rope_cols:
        in_specs += [pl.BlockSpec((tm, LANES), im(lambda bi, si: (si, 0)))] * 2
        args += list(rope)
    if norm_cols:
        seg = np.kron(np.eye(tn // HEAD_DIM), np.ones((HEAD_DIM, HEAD_DIM))).astype(np.float32)
        in_specs += [pl.BlockSpec((1, norm_cols), im(lambda bi, si: (0, 0))),
                     pl.BlockSpec((tn, tn), im(lambda bi, si: (0, 0)))]
        args += [head_gain, jnp.asarray(seg, BF16)]
    out_specs = [pl.BlockSpec((1, tm, nout), im(lambda bi, si: (bi, si, 0)))]
    out_shape = [jax.ShapeDtypeStruct((b, s, nout), BF16)]
    scratch = []
    n_prefetch = 0
    if fuse:
        y_sorted, pos, prev_mods = moe
        in_specs += [pl.BlockSpec(memory_space=pl.ANY),
                     pl.BlockSpec((1, 6, d), im(lambda bi, si: (bi, 0, 0))),
                     pl.BlockSpec((1, 6, d), im(lambda bi, si: (CTX_MOD_ROW, 0, 0)))]
        args = [pos] + args + [y_sorted, prev_mods, prev_mods]
        out_specs.append(pl.BlockSpec((1, tm, d), im(lambda bi, si: (bi, si, 0))))
        out_shape.append(jax.ShapeDtypeStruct((b, s, d), F32))
        scratch = [pltpu.VMEM((2, tm, d), F32), pltpu.SemaphoreType.DMA((2,))]
        n_prefetch = 1
    kern = functools.partial(_proj_kernel, tm=tm, tn=tn, nout=nout, ctx_rows=ctx_rows, q_cols=q_cols,
                             rope_cols=rope_cols, norm_cols=norm_cols, fuse_moe=fuse, n_s=n_s, n_tiles=b * n_s)
    res = pl.pallas_call(
        kern,
        grid_spec=pltpu.PrefetchScalarGridSpec(num_scalar_prefetch=n_prefetch, grid=(b, n_s), in_specs=in_specs,
                                               out_specs=out_specs, scratch_shapes=scratch),
        out_shape=out_shape, compiler_params=_cparams(2), name="proj",
    )(*args)
    return (res[0], res[1]) if fuse else (res[0], x)


def _lane_halves():
    lane = lax.broadcasted_iota(jnp.int32, (1, LANES), 1)
    return lane < HEAD_DIM


def _head_half(x, lo, half, fill=0.0):
    f = jnp.full_like(x, fill)
    return jnp.where(lo, x, f) if half == 0 else jnp.where(lo, f, x)


def _exp2_parts(s_list):
    m = functools.reduce(jnp.maximum, [jnp.max(s, axis=-1, keepdims=True) for s in s_list])
    return [jnp.exp2(s - m) for s in s_list]


def _key_chunks(k_ref, v_ref, nk, chunk, v_map=lambda v: v):
    return [(k_ref[0, c0:min(c0 + chunk, nk)], v_map(v_ref[0, c0:min(c0 + chunk, nk)]), None)
            for c0 in range(0, nk, chunk)]


def _online_attend(q, chunks, row_sums=False):
    m = acc = l = None
    for k, v, bias in chunks:
        s = _dot_nt(q, k)
        if bias is not None:
            s = s + bias
        mc = jnp.max(s, axis=-1, keepdims=True)
        if acc is None:
            m = mc
            e = jnp.exp2(s - m)
            acc = _dot(e.astype(BF16), v)
            if row_sums:
                l = jnp.sum(e, axis=-1, keepdims=True)
        else:
            m_new = jnp.maximum(m, mc)
            alpha = jnp.exp2(m - m_new)
            e = jnp.exp2(s - m_new)
            acc = acc * alpha + _dot(e.astype(BF16), v)
            if row_sums:
                l = l * alpha + jnp.sum(e, axis=-1, keepdims=True)
            m = m_new
    return acc, l


def _pv_sum(o, half):
    l = o[:, HEAD_DIM:HEAD_DIM + 1] if half == 0 else o[:, 0:1]
    return o * (1.0 / l)


def _diff_attn_kernel(lam_ref, q_ref, k_ref, v_ref, g_ref, o_ref, *, lam_init, tq):
    lv = lam_ref[...]
    lam = (jnp.exp(jnp.sum(lv[0:1] * lv[1:2], axis=-1, keepdims=True))
           - jnp.exp(jnp.sum(lv[2:3] * lv[3:4], axis=-1, keepdims=True)) + lam_init)
    lo = _lane_halves()

    def attend(q, nk):
        nq = q.shape[0]
        q2 = jnp.concatenate([_head_half(q, lo, 0), _head_half(q, lo, 1)], axis=0)
        acc, l = _online_attend(q2, _key_chunks(k_ref, v_ref, nk, ATTN_KCHUNK), row_sums=True)
        o = acc[:nq] * (1.0 / l[:nq]) - acc[nq:] * (lam / l[nq:])
        return (_rms(o, g_ref[...]) * (1.0 - lam_init)).astype(BF16)

    o_ref[0, :CTX_LEN] = attend(q_ref[0, :CTX_LEN], CTX_LEN)

    def tile(t, carry):
        r0 = pl.multiple_of(CTX_LEN + t * tq, math.gcd(CTX_LEN, tq))
        o_ref[0, pl.ds(r0, tq)] = attend(q_ref[0, pl.ds(r0, tq)], S_ALL)
        return carry

    lax.fori_loop(0, SEQ // tq, tile, 0, unroll=2)


def _diff_attn(qkv, lam_vecs, subln_g, lam_init):
    b = qkv.shape[0]
    h = DIFF_HEADS
    kern = functools.partial(_diff_attn_kernel, lam_init=lam_init, tq=DIFF_TQ)
    slab = lambda off: pl.BlockSpec((1, S_ALL, LANES), lambda bi, hi: (bi, 0, off + hi))
    return pl.pallas_call(
        kern, grid=(b, h),
        in_specs=[pl.BlockSpec((4, HEAD_DIM), lambda bi, hi: (0, 0)), slab(0), slab(h), slab(2 * h),
                  pl.BlockSpec((1, LANES), lambda bi, hi: (0, 0))],
        out_specs=slab(0),
        out_shape=jax.ShapeDtypeStruct((b, S_ALL, D_MODEL), BF16),
        compiler_params=_cparams(2), name="diff_attn",
    )(lam_vecs, qkv, qkv, qkv, subln_g.reshape(1, LANES))


NA_ROWS = SEQ // GRID_W
NA_QROWS = 4
NA_QBLK = NA_QROWS * GRID_W
NA_BROWS = NA_QROWS + NA_WR
NA_BKEYS = NA_BROWS * GRID_W
NA_NBLK = NA_ROWS // NA_QROWS
NA_VARIANTS = 3
NA_BASE_ROWS = 2 * NA_WR


def _na_row_geometry():
    row_idx = np.zeros((NA_VARIANTS, NA_QROWS, NA_BROWS), np.int32)
    row_valid = np.zeros((NA_VARIANTS, NA_QROWS, NA_BROWS), bool)
    for var, rb in enumerate((0, 1, NA_NBLK - 1)):
        s0 = np.clip(NA_QROWS * rb - NA_WR // 2, 0, NA_ROWS - NA_BROWS)
        for a in range(NA_QROWS):
            r = NA_QROWS * rb + a
            rs = np.clip(r - NA_WR // 2, 0, NA_ROWS - NA_WR)
            for bb in range(NA_BROWS):
                kr = s0 + bb
                row_valid[var, a, bb] = rs <= kr < rs + NA_WR
                row_idx[var, a, bb] = kr - r + NA_WR - 1
    return row_idx, row_valid


def _na_build_bias(base_ref, bias_ref):
    row_idx, row_valid = _na_row_geometry()
    qc = lax.broadcasted_iota(jnp.int32, (GRID_W, LANES), 0)
    lane = lax.broadcasted_iota(jnp.int32, (GRID_W, LANES), 1)
    kc = lane % GRID_W
    win0 = jnp.clip(qc - NA_WC // 2, 0, GRID_W - NA_WC)
    col_valid = (kc >= win0) & (kc < win0 + NA_WC)
    low = lane < GRID_W
    for h in range(2):
        for var in range(NA_VARIANTS):
            for a in range(NA_QROWS):
                for bp in range(NA_BROWS // 2):
                    v0, v1 = row_valid[var, a, 2 * bp], row_valid[var, a, 2 * bp + 1]
                    rows = slice(a * GRID_W, (a + 1) * GRID_W)
                    cols = slice(bp * LANES, (bp + 1) * LANES)
                    if not (v0 or v1):
                        bias_ref[h, var, rows, cols] = jnp.full((GRID_W, LANES), NEG_INF, F32)
                        continue
                    k = int(row_idx[var, a, 2 * bp + 1])
                    slab = jnp.broadcast_to(base_ref[h, k:k + 1, :], (GRID_W, LANES))
                    slab = pltpu.roll(slab, LANES - (NA_WC - 1), 1, stride=1, stride_axis=0)
                    if v0 and v1:
                        ok = col_valid
                    elif v0:
                        ok = jnp.logical_and(col_valid, low)
                    else:
                        ok = jnp.logical_and(col_valid, jnp.logical_not(low))
                    bias_ref[h, var, rows, cols] = jnp.where(ok, slab, NEG_INF)


def _na_attn_kernel(q_ref, k_ref, v_ref, base_ref, o_ref, bias_ref):
    @pl.when(pl.program_id(1) == 0)
    def _():
        _na_build_bias(base_ref, bias_ref)

    lo = _lane_halves()
    kc = k_ref[0, :CTX_LEN]
    vc = v_ref[0, :CTX_LEN]
    vc_aug = [_head_half(vc, lo, h, 1.0) for h in range(2)]

    q = q_ref[0, :CTX_LEN]
    outs = []
    for h in range(2):
        (e,) = _exp2_parts([_dot_nt(_head_half(q, lo, h), kc)])
        outs.append(_pv_sum(_dot(e.astype(BF16), vc_aug[h]), h))
    o_ref[0, :CTX_LEN] = jnp.where(lo, outs[0], outs[1]).astype(BF16)

    def block(rb, carry):
        s0 = jnp.clip(NA_QROWS * rb - NA_WR // 2, 0, NA_ROWS - NA_BROWS)
        var = jnp.where(rb == 0, 0, jnp.where(rb == NA_NBLK - 1, 2, 1))
        q0 = pl.multiple_of(CTX_LEN + rb * NA_QBLK, NA_QBLK)
        b0 = pl.multiple_of(CTX_LEN + s0 * GRID_W, GRID_W)
        q = q_ref[0, pl.ds(q0, NA_QBLK)]
        kb = k_ref[0, pl.ds(b0, NA_BKEYS)]
        vb = v_ref[0, pl.ds(b0, NA_BKEYS)]
        q2 = jnp.concatenate([_head_half(q, lo, 0), _head_half(q, lo, 1)], axis=0)
        bias = jnp.concatenate([bias_ref[0, var], bias_ref[1, var]], axis=0)
        acc, l = _online_attend(q2, [(kb, vb, bias), (kc, vc, None)], row_sums=True)
        o2 = acc * (1.0 / l)
        o_ref[0, pl.ds(q0, NA_QBLK)] = jnp.where(lo, o2[:NA_QBLK], o2[NA_QBLK:]).astype(BF16)
        return carry

    lax.fori_loop(0, NA_NBLK, block, 0, unroll=4)


def _na_base_rows(rpb):
    n_c = 2 * NA_WC - 1
    r = rpb.astype(F32) * LOG2E
    zrow = jnp.zeros((NA_HEADS, 1, n_c), F32)
    low = jnp.concatenate([zrow, r], axis=1)
    high = jnp.concatenate([r, zrow], axis=1)
    pad = jnp.zeros((NA_HEADS, NA_BASE_ROWS, GRID_W - n_c), F32)
    return jnp.concatenate([low, pad, high, pad], axis=2)


def _na_attn(qkv, base):
    b = qkv.shape[0]
    hp = NA_HEADS // 2
    return pl.pallas_call(
        _na_attn_kernel, grid=(hp, b),
        in_specs=[pl.BlockSpec((1, S_ALL, LANES), lambda hi, bi: (bi, 0, hi)),
                  pl.BlockSpec((1, S_ALL, LANES), lambda hi, bi: (bi, 0, hp + hi)),
                  pl.BlockSpec((1, S_ALL, LANES), lambda hi, bi: (bi, 0, 2 * hp + hi)),
                  pl.BlockSpec((2, NA_BASE_ROWS, LANES), lambda hi, bi: (hi, 0, 0))],
        out_specs=pl.BlockSpec((1, S_ALL, LANES), lambda hi, bi: (bi, 0, hi)),
        out_shape=jax.ShapeDtypeStruct((b, S_ALL, D_MODEL), BF16),
        scratch_shapes=[pltpu.VMEM((2, NA_VARIANTS, NA_QBLK, NA_BKEYS), F32)],
        compiler_params=_cparams(2), name="na_attn",
    )(qkv, qkv, qkv, base)


GQA_GROUP = GQA_Q_HEADS // GQA_KV_HEADS


def _gqa_attn_kernel(q_ref, k_ref, v_ref, o_ref, *, tq):
    lo = _lane_halves()
    outs = [[None, None] for _ in range(GQA_GROUP)]
    for j in range(2):
        qs = []
        for g in range(GQA_GROUP):
            slab, half = divmod(j * GQA_GROUP + g, 2)
            q = q_ref[0, :, slab * LANES:(slab + 1) * LANES]
            if half != j:
                q = jnp.concatenate([q[:, HEAD_DIM:], q[:, :HEAD_DIM]], axis=1)
            qs.append(_head_half(q, lo, j))
        chunks = _key_chunks(k_ref, v_ref, S_ALL, ATTN_KCHUNK, v_map=lambda v: _head_half(v, lo, j, 1.0))
        acc, _ = _online_attend(jnp.concatenate(qs, axis=0), chunks)
        o = _pv_sum(acc, j)
        for g in range(GQA_GROUP):
            slab, half = divmod(j * GQA_GROUP + g, 2)
            og = o[g * tq:(g + 1) * tq]
            if half != j:
                og = pltpu.roll(og, HEAD_DIM, 1)
            outs[slab][half] = og
    for slab in range(GQA_GROUP):
        o_ref[0, :, slab * LANES:(slab + 1) * LANES] = jnp.where(lo, outs[slab][0], outs[slab][1]).astype(BF16)


def _gqa_attn(qkv):
    b = qkv.shape[0]
    tq = 256
    qw = GQA_GROUP * 2 * HEAD_DIM
    n_pairs = GQA_KV_HEADS // 2
    kv0 = GQA_Q_HEADS * HEAD_DIM // LANES
    kern = functools.partial(_gqa_attn_kernel, tq=tq)
    return pl.pallas_call(
        kern, grid=(b, n_pairs, SEQ // tq),
        in_specs=[pl.BlockSpec((1, tq, qw), lambda bi, pi, qi: (bi, qi + CTX_LEN // tq, pi)),
                  pl.BlockSpec((1, S_ALL, LANES), lambda bi, pi, qi: (bi, 0, kv0 + pi)),
                  pl.BlockSpec((1, S_ALL, LANES), lambda bi, pi, qi: (bi, 0, kv0 + n_pairs + pi))],
        out_specs=pl.BlockSpec((1, tq, qw), lambda bi, pi, qi: (bi, qi, pi)),
        out_shape=jax.ShapeDtypeStruct((b, SEQ, D_MODEL), BF16),
        compiler_params=_cparams(3), name="gqa_attn",
    )(qkv, qkv, qkv)


def _conv_kernel(gb_ref, gc_ref, u_ref, w_ref, o_ref):
    p = gc_ref[0].astype(F32) * u_ref[0].astype(F32)
    n = p.shape[0]
    rows = lax.broadcasted_iota(jnp.int32, (n, 1), 0)
    prev = jnp.where(rows == 0, 0.0, pltpu.roll(p, 1, 0))
    nxt = jnp.where(rows == n - 1, 0.0, pltpu.roll(p, n - 1, 0))
    w = w_ref[...]
    z = prev * w[0:1] + p * w[1:2] + nxt * w[2:3]
    o_ref[0] = (gb_ref[0].astype(F32) * z).astype(BF16)


def _conv_mix(proj, w_conv):
    b, n, _ = proj.shape
    nb = D_MODEL // LANES
    spec = lambda off: pl.BlockSpec((1, n, LANES), lambda bi, ci: (bi, 0, off + ci))
    return pl.pallas_call(
        _conv_kernel, grid=(b, nb),
        in_specs=[spec(0), spec(nb), spec(2 * nb), pl.BlockSpec((3, LANES), lambda bi, ci: (0, ci))],
        out_specs=pl.BlockSpec((1, n, LANES), lambda bi, ci: (bi, 0, ci)),
        out_shape=jax.ShapeDtypeStruct((b, n, D_MODEL), BF16),
        compiler_params=_cparams(2), name="conv_mix",
    )(proj, proj, proj, w_conv)


def _route(logits_t, bias):
    scores = _sigmoid(logits_t)
    sel = scores + bias
    nt = sel.shape[1]
    idx = lax.broadcasted_iota(jnp.int32, (N_EXPERTS, nt), 0).astype(F32)
    best = gidx = None
    for g in range(N_GROUPS):
        r = [sel[g * GROUP_SIZE + i:g * GROUP_SIZE + i + 1] for i in range(GROUP_SIZE)]
        pair_sums = [r[i] + r[j] for i in range(GROUP_SIZE) for j in range(i + 1, GROUP_SIZE)]
        gs = functools.reduce(jnp.maximum, pair_sums)
        if g == 0:
            best, gidx = gs, jnp.zeros_like(gs)
        else:
            upd = gs > best
            best = jnp.where(upd, gs, best)
            gidx = jnp.where(upd, float(g), gidx)
    in_group = jnp.floor(idx * (1.0 / GROUP_SIZE)) == gidx
    masked = jnp.where(in_group, sel, NEG_INF)
    m1 = jnp.max(masked, axis=0, keepdims=True)
    i1 = jnp.min(jnp.where(masked == m1, idx, float(N_EXPERTS)), axis=0, keepdims=True)
    masked2 = jnp.where(idx == i1, -jnp.inf, masked)
    m2 = jnp.max(masked2, axis=0, keepdims=True)
    i2 = jnp.min(jnp.where(masked2 == m2, idx, float(N_EXPERTS)), axis=0, keepdims=True)
    w1 = jnp.sum(jnp.where(idx == i1, scores, 0.0), axis=0, keepdims=True)
    w2 = jnp.sum(jnp.where(idx == i2, scores, 0.0), axis=0, keepdims=True)
    inv = 1.0 / (w1 + w2)
    gates = jnp.where(idx == i1, w1 * inv, 0.0) + jnp.where(idx == i2, w2 * inv, 0.0)
    rows = []
    for i in range(GROUP_SIZE):
        rows.append(functools.reduce(
            lambda a, b: a + b,
            [jnp.where(gidx == float(g), gates[g * GROUP_SIZE + i:g * GROUP_SIZE + i + 1], 0.0) for g in range(N_GROUPS)]))
    rows.append(gidx)
    rows += [jnp.zeros_like(gidx)] * (ROUTE_ROWS - len(rows))
    return jnp.concatenate(rows, axis=0)


def _out_kernel(o_ref, x_ref, mb_ref, mc_ref, g_ref, wo_ref, rwh_ref, rwl_ref, rb_ref,
                xn_ref, h_ref, rt_ref, *, tm, ctx_rows):
    row0 = pl.program_id(1) * tm
    mod = lambda k: _row_mod(mb_ref, mc_ref, k, row0, tm, ctx_rows)
    x = x_ref[0] + mod(2) * _dot(o_ref[0], wo_ref[...])
    xn_ref[0] = x
    h = _norm_mod(x, g_ref[...], mod(3), mod(4))
    hb = h.astype(BF16)
    hl = (h - hb.astype(F32)).astype(BF16)
    logits_t = _dot_nt(rwh_ref[...], hb) + _dot_nt(rwh_ref[...], hl) + _dot_nt(rwl_ref[...], hb)
    route = _route(logits_t, rb_ref[...])
    rt_ref[0] = route
    h_ref[0, :, :D_MODEL] = h
    h_ref[0, :, D_MODEL:] = jnp.concatenate([route, jnp.zeros((LANES - ROUTE_ROWS, tm), F32)], axis=0).T


def _out_proj(o, x, mods, g, w_o, rw_hi, rw_lo, rbias, *, tm, ctx_rows, x_row_off=0):
    b, s, d = o.shape
    assert s % tm == 0 and x_row_off % tm == 0
    xo = x_row_off // tm
    kern = functools.partial(_out_kernel, tm=tm, ctx_rows=ctx_rows)
    full = lambda shape: pl.BlockSpec(shape, lambda bi, si: (0,) * len(shape))
    return pl.pallas_call(
        kern, grid=(b, s // tm),
        in_specs=[pl.BlockSpec((1, tm, d), lambda bi, si: (bi, si, 0)),
                  pl.BlockSpec((1, tm, d), lambda bi, si: (bi, si + xo, 0)),
                  pl.BlockSpec((1, 6, d), lambda bi, si: (bi, 0, 0)),
                  pl.BlockSpec((1, 6, d), lambda bi, si: (CTX_MOD_ROW, 0, 0)),
                  full((1, d)), full((d, d)), full((N_EXPERTS, d)), full((N_EXPERTS, d)), full((N_EXPERTS, 1))],
        out_specs=[pl.BlockSpec((1, tm, d), lambda bi, si: (bi, si, 0)),
                   pl.BlockSpec((1, tm, H_EXT), lambda bi, si: (bi, si, 0)),
                   pl.BlockSpec((1, ROUTE_ROWS, tm), lambda bi, si: (bi, 0, si))],
        out_shape=[jax.ShapeDtypeStruct((b, s, d), F32),
                   jax.ShapeDtypeStruct((b, s, H_EXT), F32),
                   jax.ShapeDtypeStruct((b, ROUTE_ROWS, s), F32)],
        compiler_params=_cparams(2), name="out_proj",
    )(o, x, mods, mods, g.reshape(1, d), w_o, rw_hi, rw_lo, rbias)


def _moe_kernel(pos_ref, tgroup_ref, nact_ref, h_hbm, zeros_hbm, wg_ref, wu_ref, wd_ref, o_ref, perm, buf, sem, zsem,
                *, tm, n_tok):
    i = pl.program_id(0)
    n_act = nact_ref[0]

    @pl.when(i == 0)
    def _():
        clear = pltpu.make_async_copy(zeros_hbm, perm, zsem.at[0])
        clear.start()
        clear.wait()

        def inv(t, carry):
            perm[pos_ref[t]] = t
            return carry

        lax.fori_loop(0, n_tok, inv, 0, unroll=8)
        _start_rows_loop(lambda j: perm[j], h_hbm, buf, sem, 0, tm)

    @pl.when(i < n_act)
    def _():
        slot = i % 2
        _wait_rows(h_hbm, buf, sem, slot, tm)
        @pl.when(i + 1 < n_act)
        def _():
            nxt = (i + 1) * tm
            _start_rows_loop(lambda j: perm[nxt + j], h_hbm, buf, sem, 1 - slot, tm)

        hb = buf[slot, :, :D_MODEL].astype(BF16)
        gates = buf[slot, :, D_MODEL:]
        acc = jnp.zeros((tm, D_MODEL), F32)
        for j in range(GROUP_SIZE):
            g = _dot(hb, wg_ref[j].astype(BF16))
            u = _dot(hb, wu_ref[j].astype(BF16))
            a = (g * _sigmoid(g)) * u * gates[:, j:j + 1]
            acc = acc + _dot(a.astype(BF16), wd_ref[j].astype(BF16))
        o_ref[...] = acc

    @pl.when(i >= n_act)
    def _():
        o_ref[...] = jnp.zeros_like(o_ref)


def _moe(h_ext, pos, tile_group, n_active, layer, wg, wu, wd):
    tm = MOE_TM
    n_tok = pos.shape[0]
    n_tiles = tile_group.shape[0]
    d = D_MODEL
    wspec = lambda shape: pl.BlockSpec((None,) + shape, lambda i, ps, tg, na: (layer, tg[i], 0, 0),
                                       pipeline_mode=pl.Buffered(1))
    kern = functools.partial(_moe_kernel, tm=tm, n_tok=n_tok)
    return pl.pallas_call(
        kern,
        grid_spec=pltpu.PrefetchScalarGridSpec(
            num_scalar_prefetch=3, grid=(n_tiles,),
            in_specs=[pl.BlockSpec(memory_space=pl.ANY), pl.BlockSpec(memory_space=pl.ANY),
                      wspec((GROUP_SIZE, d, EXPERT_FF)), wspec((GROUP_SIZE, d, EXPERT_FF)),
                      wspec((GROUP_SIZE, EXPERT_FF, d))],
            out_specs=pl.BlockSpec((tm, d), lambda i, *_: (i, 0)),
            scratch_shapes=[pltpu.SMEM((n_tiles * tm,), jnp.int32), pltpu.VMEM((2, tm, H_EXT), F32),
                            pltpu.SemaphoreType.DMA((2,)), pltpu.SemaphoreType.DMA((1,))]),
        out_shape=jax.ShapeDtypeStruct((n_tiles * tm, d), F32),
        compiler_params=_cparams(1), name="moe",
    )(pos, tile_group, n_active, h_ext, jnp.zeros((n_tiles * tm,), jnp.int32), wg, wu, wd)


def _moe_plan(route, tm):
    b, _, s = route.shape
    t = b * s
    n_tiles = t // tm + N_GROUPS
    gid = route[:, GROUP_SIZE, :].reshape(t).astype(jnp.int32)
    onehot = (gid[:, None] == jnp.arange(N_GROUPS, dtype=jnp.int32)[None, :]).astype(jnp.int32)
    csum = jnp.cumsum(onehot, axis=0)
    counts = csum[-1]
    in_group_rank = jnp.sum((csum - onehot) * onehot, axis=1)
    tiles = (counts + tm - 1) // tm
    tile_end = jnp.cumsum(tiles)
    tile_start = tile_end - tiles
    pos = jnp.sum(onehot * tile_start[None, :], axis=1) * tm + in_group_rank
    tile_ids = jnp.arange(n_tiles, dtype=jnp.int32)
    tile_group = jnp.minimum(jnp.sum((tile_ids[:, None] >= tile_end[None, :]).astype(jnp.int32), axis=1), N_GROUPS - 1)
    i32 = lambda a: a.astype(jnp.int32)
    return i32(pos), i32(tile_group), i32(tile_end[-1:])


def _combine_kernel(pos_ref, x_ref, y_hbm, mb_ref, mc_ref, fg_ref, o_ref, buf, sem, *, tm, ctx_rows, tiles_per_sample, n_tiles):
    i = pl.program_id(0)
    slot = i % 2

    @pl.when(i == 0)
    def _():
        _start_rows_loop(lambda j: pos_ref[j], y_hbm, buf, sem, 0, tm)

    @pl.when(i + 1 < n_tiles)
    def _():
        nxt = (i + 1) * tm
        _start_rows_loop(lambda j: pos_ref[nxt + j], y_hbm, buf, sem, 1 - slot, tm)

    _wait_rows(y_hbm, buf, sem, slot, tm)
    row0 = (i % tiles_per_sample) * tm
    x = x_ref[...] + _row_mod(mb_ref, mc_ref, 5, row0, tm, ctx_rows) * buf[slot]
    o_ref[...] = _rms(x, fg_ref[...])


def _combine_final(x, y_sorted, pos, mods, final_g, *, tm, ctx_rows, s):
    t, d = x.shape
    tps = s // tm
    n_tiles = t // tm
    kern = functools.partial(_combine_kernel, tm=tm, ctx_rows=ctx_rows, tiles_per_sample=tps, n_tiles=n_tiles)
    return pl.pallas_call(
        kern,
        grid_spec=pltpu.PrefetchScalarGridSpec(
            num_scalar_prefetch=1, grid=(n_tiles,),
            in_specs=[pl.BlockSpec((tm, d), lambda i, pos: (i, 0)),
                      pl.BlockSpec(memory_space=pl.ANY),
                      pl.BlockSpec((1, 6, d), lambda i, pos: (i // tps, 0, 0)),
                      pl.BlockSpec((1, 6, d), lambda i, pos: (CTX_MOD_ROW, 0, 0)),
                      pl.BlockSpec((1, d), lambda i, pos: (0, 0))],
            out_specs=pl.BlockSpec((tm, d), lambda i, pos: (i, 0)),
            scratch_shapes=[pltpu.VMEM((2, tm, d), F32), pltpu.SemaphoreType.DMA((2,))]),
        out_shape=jax.ShapeDtypeStruct((t, d), F32),
        compiler_params=_cparams(1), name="combine",
    )(pos, x, y_sorted, mods, mods, final_g.reshape(1, d))


def _rope_tables():
    t = jnp.arange(SEQ)
    row = (t // GRID_W).astype(F32)
    col = (t % GRID_W).astype(F32)
    quarter = HEAD_DIM // 4
    inv_freq = ROPE_THETA ** (-jnp.arange(quarter, dtype=F32) / quarter)
    ang = jnp.concatenate([row[:, None] * inv_freq, col[:, None] * inv_freq], axis=-1)
    cos, sin = jnp.cos(ang), jnp.sin(ang)
    reps = LANES // HEAD_DIM
    cos_l = jnp.tile(jnp.concatenate([cos, cos], axis=-1), (1, reps))
    sin_l = jnp.tile(jnp.concatenate([-sin, sin], axis=-1), (1, reps))
    cos_all = jnp.concatenate([jnp.ones((CTX_LEN, LANES), F32), cos_l], axis=0)
    sin_all = jnp.concatenate([jnp.zeros((CTX_LEN, LANES), F32), sin_l], axis=0)
    return cos_all, sin_all


def kernel(x, c, ctx, c_ctx, ada_w, ada_b, norm1_g, norm2_g, final_norm_g, diff_w_qkv, diff_w_o, diff_lambda_q1, diff_lambda_k1, diff_lambda_q2, diff_lambda_k2, diff_subln_g, na_w_qkv, na_w_o, na_rpb, gqa_w_qkv, gqa_w_o, gqa_q_norm_g, gqa_k_norm_g, conv_w_in, conv_w, conv_w_out, router_w, router_bias, moe_w_gate, moe_w_up, moe_w_down):
    assert DEPTH % N_MIXERS == 0
    b = x.shape[0]
    cc = jnp.zeros((MOD_ROWS, D_MODEL), F32).at[:b].set(c).at[CTX_MOD_ROW].set(c_ctx)
    mods_all = _ada(cc, ada_w, ada_b).reshape(DEPTH, MOD_ROWS, 6, D_MODEL)
    rope = _rope_tables()
    rw_t = router_w.T
    rw_hi = rw_t.astype(BF16)
    rw_lo = (rw_t - rw_hi.astype(F32)).astype(BF16)
    rbias = router_bias.reshape(N_EXPERTS, 1).astype(F32)
    wg_all, wu_all, wd_all = moe_w_gate, moe_w_up, moe_w_down

    xs = jnp.concatenate([ctx, x], axis=1)
    moe_pending = None
    for i in range(DEPTH):
        m, j = i % N_MIXERS, i // N_MIXERS
        mods = mods_all[i]
        has_ctx = xs.shape[1] == S_ALL
        ctx_rows = CTX_LEN if has_ctx else 0
        ctx_advances = any(k % N_MIXERS != 3 for k in range(i + 1, DEPTH))
        common = dict(tm=768 if has_ctx else 512, ctx_rows=ctx_rows, moe=moe_pending)
        if m == 0:
            lam_init = 0.8 - 0.6 * math.exp(-0.3 * i)
            qkv, xs = _proj(xs, mods, norm1_g[i], diff_w_qkv[j].astype(BF16), q_cols=D_MODEL,
                            rope=rope, rope_cols=2 * D_MODEL, **common)
            lam_vecs = jnp.stack([diff_lambda_q1[j], diff_lambda_k1[j], diff_lambda_q2[j], diff_lambda_k2[j]])
            o = _diff_attn(qkv, lam_vecs, diff_subln_g[j], lam_init)
            w_o = diff_w_o[j]
        elif m == 1:
            qkv, xs = _proj(xs, mods, norm1_g[i], na_w_qkv[j].astype(BF16), q_cols=D_MODEL, **common)
            o = _na_attn(qkv, _na_base_rows(na_rpb[j]))
            w_o = na_w_o[j]
        elif m == 2:
            qk_cols = (GQA_Q_HEADS + GQA_KV_HEADS) * HEAD_DIM
            gain = jnp.concatenate([jnp.tile(gqa_q_norm_g[j] * QK_SCALE, GQA_Q_HEADS),
                                    jnp.tile(gqa_k_norm_g[j], GQA_KV_HEADS)]).reshape(1, qk_cols)
            qkv, xs = _proj(xs, mods, norm1_g[i], gqa_w_qkv[j].astype(BF16),
                            rope=rope, rope_cols=qk_cols, head_gain=gain, norm_cols=qk_cols, **common)
            o = _gqa_attn(qkv)
            w_o = gqa_w_o[j]
        else:
            proj, xs = _proj(xs, mods, norm1_g[i], conv_w_in[j].astype(BF16), **common)
            o = _conv_mix(proj, conv_w[j])
            w_o = conv_w_out[j]

        if has_ctx and not ctx_advances:
            tm_out, x_off, out_ctx_rows = CTX_LEN, CTX_LEN, 0
            if o.shape[1] == S_ALL:
                o = o[:, CTX_LEN:]
        elif has_ctx:
            tm_out, x_off, out_ctx_rows = 768, 0, CTX_LEN
        else:
            tm_out, x_off, out_ctx_rows = 1024, 0, 0
        xs, h_ext, route = _out_proj(o, xs, mods, norm2_g[i], w_o.astype(BF16), rw_hi, rw_lo, rbias,
                                     tm=tm_out, ctx_rows=out_ctx_rows, x_row_off=x_off)
        s = xs.shape[1]
        pos, tile_group, n_active = _moe_plan(route, MOE_TM)
        y_sorted = _moe(h_ext.reshape(b * s, H_EXT), pos, tile_group, n_active, i, wg_all, wu_all, wd_all)
        moe_pending = (y_sorted, pos, mods)
    y_sorted, pos, mods = moe_pending
    s = xs.shape[1]
    return _combine_final(xs.reshape(b * s, D_MODEL), y_sorted, pos, mods, final_norm_g,
                          tm=1024, ctx_rows=0, s=s).reshape(b, s, D_MODEL)
```

```python
import functools
import math

import numpy as np
import jax
import jax.numpy as jnp
from jax import lax
from jax.experimental import pallas as pl
from jax.experimental.pallas import tpu as pltpu

D_MODEL = 1024
BATCH = 8
SEQ = 2048
DEPTH = 4
GRID_W = 64
CTX_LEN = 256
S_ALL = CTX_LEN + SEQ
N_MIXERS = 4
HEAD_DIM = 64
ATTN_SCALE = HEAD_DIM ** -0.5
LOG2E = math.log2(math.e)
QK_SCALE = ATTN_SCALE * LOG2E
DIFF_HEADS = D_MODEL // (2 * HEAD_DIM)
NA_HEADS = D_MODEL // HEAD_DIM
NA_WR = 8
NA_WC = 16
GQA_Q_HEADS = D_MODEL // HEAD_DIM
GQA_KV_HEADS = 4
ROPE_THETA = 10000.0
N_EXPERTS = 16
N_GROUPS = 4
GROUP_SIZE = N_EXPERTS // N_GROUPS
EXPERT_FF = 512
_PAIRS = [(a, b) for a in range(GROUP_SIZE) for b in range(a + 1, GROUP_SIZE)]
N_PAIRS = len(_PAIRS)
N_CLASSES = N_GROUPS * N_PAIRS
NORM_EPS = 1e-6
NEG_INF = -1e30

LANES = 128
SUBLANES = 8
MOD_ROWS = 16
CTX_MOD_ROW = BATCH
H_EXT = D_MODEL
ROUTE_ROWS = SUBLANES
MOE_TM = 512
MAX_DMA_BURST = 64
ATTN_KCHUNK = 768
DIFF_TQ = 512
F32 = jnp.float32
BF16 = jnp.bfloat16
VMEM_LIMIT = 56 * 1024 * 1024

_NT = (((1,), (1,)), ((), ()))


def _cparams(n_axes):
    return pltpu.CompilerParams(dimension_semantics=("arbitrary",) * n_axes,
                                vmem_limit_bytes=VMEM_LIMIT)


def _dot(a, b):
    return jnp.dot(a, b, preferred_element_type=F32)


def _dot_nt(a, b):
    return lax.dot_general(a, b, _NT, preferred_element_type=F32)


def _sigmoid(x):
    return 1.0 / (1.0 + jnp.exp(-x))


def _row_mod(mb_ref, mc_ref, k, row0, tm, ctx_rows):
    mb = mb_ref[0, k:k + 1, :]
    if ctx_rows == 0:
        return mb
    rows = row0 + lax.broadcasted_iota(jnp.int32, (tm, 1), 0)
    return jnp.where(rows < ctx_rows, mc_ref[0, k:k + 1, :], mb)


def _rms(x, g):
    return x * lax.rsqrt(jnp.mean(x * x, axis=-1, keepdims=True) + NORM_EPS) * g


def _norm_mod(x, g, shift, scale):
    return _rms(x, g) * (1.0 + scale) + shift


def _row_copy(src_hbm, buf, sem, slot, src_row, dst_row):
    return pltpu.make_async_copy(src_hbm.at[pl.ds(src_row, 1)], buf.at[slot, pl.ds(dst_row, 1)], sem.at[slot])


def _wait_rows(src_hbm, buf, sem, slot, n):
    pltpu.make_async_copy(src_hbm.at[pl.ds(0, n)], buf.at[slot], sem.at[slot]).wait()


def _start_rows_loop(row_of, src_hbm, buf, sem, slot, n):
    def body(j, carry):
        _row_copy(src_hbm, buf, sem, slot, row_of(j), j).start()
        return carry

    lax.fori_loop(0, n, body, 0, unroll=8)


def _split_rows(n, parts):
    edges = [round(k * n / parts) for k in range(parts + 1)]
    return list(zip(edges[:-1], edges[1:]))


class _RowPrefetch:
    def __init__(self, row_of, src_hbm, buf, sem, slot, n, parts):
        self.args = (row_of, src_hbm, buf, sem, slot)
        self.pieces = _split_rows(n, parts)
        self.k = 0

    def issue_piece(self):
        row_of, src_hbm, buf, sem, slot = self.args
        lo, hi = self.pieces[self.k]
        self.k += 1
        for j in range(lo, hi):
            _row_copy(src_hbm, buf, sem, slot, row_of(j), j).start()


def _ada_kernel(cc_ref, w_ref, b_ref, o_ref):
    cc = cc_ref[...]
    s = (cc * _sigmoid(cc)).astype(BF16)
    o_ref[0] = _dot(s, w_ref[0].astype(BF16)) + b_ref[0]


def _ada(cc, ada_w, ada_b):
    tn = 1536
    n = 6 * D_MODEL
    return pl.pallas_call(
        _ada_kernel,
        grid=(DEPTH, n // tn),
        in_specs=[pl.BlockSpec((MOD_ROWS, D_MODEL), lambda l, j: (0, 0)),
                  pl.BlockSpec((1, D_MODEL, tn), lambda l, j: (l, 0, j)),
                  pl.BlockSpec((1, 1, tn), lambda l, j: (l, 0, j))],
        out_specs=pl.BlockSpec((1, MOD_ROWS, tn), lambda l, j: (l, 0, j)),
        out_shape=jax.ShapeDtypeStruct((DEPTH, MOD_ROWS, n), F32),
        compiler_params=_cparams(2), name="ada",
    )(cc, ada_w, ada_b.reshape(DEPTH, 1, n))


def _rope(x, cos, sin_signed):
    lane = lax.broadcasted_iota(jnp.int32, (1, LANES), 1)
    first_half = (lane % HEAD_DIM) < (HEAD_DIM // 2)
    partner = jnp.where(first_half, pltpu.roll(x, LANES - HEAD_DIM // 2, 1), pltpu.roll(x, HEAD_DIM // 2, 1))
    return x * cos + partner * sin_signed


def _proj_kernel(*refs, tm, tn, nout, ctx_rows, q_cols, rope_cols, norm_cols, fuse_moe, n_s, n_tiles):
    refs = list(refs)
    pos_ref = refs.pop(0) if fuse_moe else None
    x_ref, mb_ref, mc_ref, g_ref, w_ref = refs[:5]
    rest = refs[5:]
    cos_ref = sin_ref = hg_ref = seg_ref = y_hbm = pmb_ref = pmc_ref = None
    if rope_cols:
        cos_ref, sin_ref = rest[:2]
        rest = rest[2:]
    if norm_cols:
        hg_ref, seg_ref = rest[:2]
        rest = rest[2:]
    if fuse_moe:
        y_hbm, pmb_ref, pmc_ref = rest[:3]
        o_ref, xn_ref, buf, sem = rest[3:]
    else:
        (o_ref,) = rest

    row0 = pl.program_id(1) * tm
    x = x_ref[0]
    prefetch = None
    n_chunks = nout // tn
    pieces_per_chunk = -(-tm // (n_chunks * MAX_DMA_BURST))
    if fuse_moe:
        lin = pl.program_id(0) * n_s + pl.program_id(1)
        slot = lin % 2

        @pl.when(lin == 0)
        def _():
            _start_rows_loop(lambda j: pos_ref[j], y_hbm, buf, sem, 0, tm)

        _wait_rows(y_hbm, buf, sem, slot, tm)
        x = x + _row_mod(pmb_ref, pmc_ref, 5, row0, tm, ctx_rows) * buf[slot]
        xn_ref[0] = x
        nxt = jnp.minimum(lin + 1, n_tiles - 1) * tm
        prefetch = _RowPrefetch(lambda j: pos_ref[nxt + j], y_hbm, buf, sem, 1 - slot, tm, n_chunks * pieces_per_chunk)

    shift = _row_mod(mb_ref, mc_ref, 0, row0, tm, ctx_rows)
    scale = _row_mod(mb_ref, mc_ref, 1, row0, tm, ctx_rows)
    hb = _norm_mod(x, g_ref[...], shift, scale).astype(BF16)
    for c in range(n_chunks):
        lo = c * tn
        acc = _dot(hb, w_ref[:, lo:lo + tn])
        if prefetch is not None:
            prefetch.issue_piece()
        if lo < q_cols:
            acc = acc * QK_SCALE
        if lo < norm_cols:
            sq = acc * acc
            sq_hi = sq.astype(BF16)
            sq_lo = (sq - sq_hi.astype(F32)).astype(BF16)
            ss = _dot(sq_hi, seg_ref[...]) + _dot(sq_lo, seg_ref[...])
            acc = acc * lax.rsqrt(ss * (1.0 / HEAD_DIM) + NORM_EPS) * hg_ref[:, lo:lo + tn]
        if lo < rope_cols:
            cos, sin = cos_ref[...], sin_ref[...]
            acc = jnp.concatenate(
                [_rope(acc[:, k * LANES:(k + 1) * LANES], cos, sin) for k in range(tn // LANES)], axis=1)
        o_ref[0, :, lo:lo + tn] = acc.astype(BF16)
        if prefetch is not None:
            for _ in range(pieces_per_chunk - 1):
                prefetch.issue_piece()

    if fuse_moe:
        @pl.when(lin == n_tiles - 1)
        def _():
            _wait_rows(y_hbm, buf, sem, 1 - slot, tm)


def _proj(x, mods, g, w, *, tm, ctx_rows, q_cols=0, rope=None, rope_cols=0, head_gain=None, norm_cols=0, moe=None):
    b, s, d = x.shape
    nout = w.shape[1]
    tn = 256
    n_s = s // tm
    fuse = moe is not None
    assert s % tm == 0 and nout % tn == 0 and rope_cols % tn == 0 and norm_cols % tn == 0 and q_cols % tn == 0
    im = lambda f: (lambda bi, si, *_: f(bi, si))
    in_specs = [pl.BlockSpec((1, tm, d), im(lambda bi, si: (bi, si, 0))),
                pl.BlockSpec((1, 6, d), im(lambda bi, si: (bi, 0, 0))),
                pl.BlockSpec((1, 6, d), im(lambda bi, si: (CTX_MOD_ROW, 0, 0))),
                pl.BlockSpec((1, d), im(lambda bi, si: (0, 0))),
                pl.BlockSpec((d, nout), im(lambda bi, si: (0, 0)))]
    args = [x, mods, mods, g.reshape(1, d), w]
    if rope_cols:
        in_specs += [pl.BlockSpec((tm, LANES), im(lambda bi, si: (si, 0)))] * 2
        args += list(rope)
    if norm_cols:
        seg = np.kron(np.eye(tn // HEAD_DIM), np.ones((HEAD_DIM, HEAD_DIM))).astype(np.float32)
        in_specs += [pl.BlockSpec((1, norm_cols), im(lambda bi, si: (0, 0))),
                     pl.BlockSpec((tn, tn), im(lambda bi, si: (0, 0)))]
        args += [head_gain, jnp.asarray(seg, BF16)]
    out_specs = [pl.BlockSpec((1, tm, nout), im(lambda bi, si: (bi, si, 0)))]
    out_shape = [jax.ShapeDtypeStruct((b, s, nout), BF16)]
    scratch = []
    n_prefetch = 0
    if fuse:
        y_sorted, pos, prev_mods = moe
        in_specs += [pl.BlockSpec(memory_space=pl.ANY),
                     pl.BlockSpec((1, 6, d), im(lambda bi, si: (bi, 0, 0))),
                     pl.BlockSpec((1, 6, d), im(lambda bi, si: (CTX_MOD_ROW, 0, 0)))]
        args = [pos] + args + [y_sorted, prev_mods, prev_mods]
        out_specs.append(pl.BlockSpec((1, tm, d), im(lambda bi, si: (bi, si, 0))))
        out_shape.append(jax.ShapeDtypeStruct((b, s, d), F32))
        scratch = [pltpu.VMEM((2, tm, d), F32), pltpu.SemaphoreType.DMA((2,))]
        n_prefetch = 1
    kern = functools.partial(_proj_kernel, tm=tm, tn=tn, nout=nout, ctx_rows=ctx_rows, q_cols=q_cols,
                             rope_cols=rope_cols, norm_cols=norm_cols, fuse_moe=fuse, n_s=n_s, n_tiles=b * n_s)
    res = pl.pallas_call(
        kern,
        grid_spec=pltpu.PrefetchScalarGridSpec(num_scalar_prefetch=n_prefetch, grid=(b, n_s), in_specs=in_specs,
                                               out_specs=out_specs, scratch_shapes=scratch),
        out_shape=out_shape, compiler_params=_cparams(2), name="proj",
    )(*args)
    return (res[0], res[1]) if fuse else (res[0], x)


def _lane_halves():
    lane = lax.broadcasted_iota(jnp.int32, (1, LANES), 1)
    return lane < HEAD_DIM


def _head_half(x, lo, half, fill=0.0):
    f = jnp.full_like(x, fill)
    return jnp.where(lo, x, f) if half == 0 else jnp.where(lo, f, x)


def _exp2_parts(s_list):
    m = functools.reduce(jnp.maximum, [jnp.max(s, axis=-1, keepdims=True) for s in s_list])
    return [jnp.exp2(s - m) for s in s_list]


def _key_chunks(k_ref, v_ref, nk, chunk, v_map=lambda v: v):
    return [(k_ref[0, c0:min(c0 + chunk, nk)], v_map(v_ref[0, c0:min(c0 + chunk, nk)]), None)
            for c0 in range(0, nk, chunk)]


def _online_attend(q, chunks, row_sums=False):
    m = acc = l = None
    for k, v, bias in chunks:
        s = _dot_nt(q, k)
        if bias is not None:
            s = s + bias
        mc = jnp.max(s, axis=-1, keepdims=True)
        if acc is None:
            m = mc
            e = jnp.exp2(s - m)
            acc = _dot(e.astype(BF16), v)
            if row_sums:
                l = jnp.sum(e, axis=-1, keepdims=True)
        else:
            m_new = jnp.maximum(m, mc)
            alpha = jnp.exp2(m - m_new)
            e = jnp.exp2(s - m_new)
            acc = acc * alpha + _dot(e.astype(BF16), v)
            if row_sums:
                l = l * alpha + jnp.sum(e, axis=-1, keepdims=True)
            m = m_new
    return acc, l


def _pv_sum(o, half):
    l = o[:, HEAD_DIM:HEAD_DIM + 1] if half == 0 else o[:, 0:1]
    return o * (1.0 / l)


def _diff_attn_kernel(lam_ref, q_ref, k_ref, v_ref, g_ref, o_ref, *, lam_init, tq):
    lv = lam_ref[...]
    lam = (jnp.exp(jnp.sum(lv[0:1] * lv[1:2], axis=-1, keepdims=True))
           - jnp.exp(jnp.sum(lv[2:3] * lv[3:4], axis=-1, keepdims=True)) + lam_init)
    lo = _lane_halves()

    def attend(q, nk):
        nq = q.shape[0]
        q2 = jnp.concatenate([_head_half(q, lo, 0), _head_half(q, lo, 1)], axis=0)
        acc, l = _online_attend(q2, _key_chunks(k_ref, v_ref, nk, ATTN_KCHUNK), row_sums=True)
        o = acc[:nq] * (1.0 / l[:nq]) - acc[nq:] * (lam / l[nq:])
        return (_rms(o, g_ref[...]) * (1.0 - lam_init)).astype(BF16)

    o_ref[0, :CTX_LEN] = attend(q_ref[0, :CTX_LEN], CTX_LEN)

    def tile(t, carry):
        r0 = pl.multiple_of(CTX_LEN + t * tq, math.gcd(CTX_LEN, tq))
        o_ref[0, pl.ds(r0, tq)] = attend(q_ref[0, pl.ds(r0, tq)], S_ALL)
        return carry

    lax.fori_loop(0, SEQ // tq, tile, 0, unroll=2)


def _diff_attn(qkv, lam_vecs, subln_g, lam_init):
    b = qkv.shape[0]
    h = DIFF_HEADS
    kern = functools.partial(_diff_attn_kernel, lam_init=lam_init, tq=DIFF_TQ)
    slab = lambda off: pl.BlockSpec((1, S_ALL, LANES), lambda bi, hi: (bi, 0, off + hi))
    return pl.pallas_call(
        kern, grid=(b, h),
        in_specs=[pl.BlockSpec((4, HEAD_DIM), lambda bi, hi: (0, 0)), slab(0), slab(h), slab(2 * h),
                  pl.BlockSpec((1, LANES), lambda bi, hi: (0, 0))],
        out_specs=slab(0),
        out_shape=jax.ShapeDtypeStruct((b, S_ALL, D_MODEL), BF16),
        compiler_params=_cparams(2), name="diff_attn",
    )(lam_vecs, qkv, qkv, qkv, subln_g.reshape(1, LANES))


NA_ROWS = SEQ // GRID_W
NA_QROWS = 4
NA_QBLK = NA_QROWS * GRID_W
NA_BROWS = NA_QROWS + NA_WR
NA_BKEYS = NA_BROWS * GRID_W
NA_NBLK = NA_ROWS // NA_QROWS
NA_VARIANTS = 3
NA_BASE_ROWS = 2 * NA_WR


def _na_row_geometry():
    row_idx = np.zeros((NA_VARIANTS, NA_QROWS, NA_BROWS), np.int32)
    row_valid = np.zeros((NA_VARIANTS, NA_QROWS, NA_BROWS), bool)
    for var, rb in enumerate((0, 1, NA_NBLK - 1)):
        s0 = np.clip(NA_QROWS * rb - NA_WR // 2, 0, NA_ROWS - NA_BROWS)
        for a in range(NA_QROWS):
            r = NA_QROWS * rb + a
            rs = np.clip(r - NA_WR // 2, 0, NA_ROWS - NA_WR)
            for bb in range(NA_BROWS):
                kr = s0 + bb
                row_valid[var, a, bb] = rs <= kr < rs + NA_WR
                row_idx[var, a, bb] = kr - r + NA_WR - 1
    return row_idx, row_valid


def _na_build_bias(base_ref, bias_ref):
    row_idx, row_valid = _na_row_geometry()
    qc = lax.broadcasted_iota(jnp.int32, (GRID_W, LANES), 0)
    lane = lax.broadcasted_iota(jnp.int32, (GRID_W, LANES), 1)
    kc = lane % GRID_W
    win0 = jnp.clip(qc - NA_WC // 2, 0, GRID_W - NA_WC)
    col_valid = (kc >= win0) & (kc < win0 + NA_WC)
    low = lane < GRID_W
    for h in range(2):
        for var in range(NA_VARIANTS):
            for a in range(NA_QROWS):
                for bp in range(NA_BROWS // 2):
                    v0, v1 = row_valid[var, a, 2 * bp], row_valid[var, a, 2 * bp + 1]
                    rows = slice(a * GRID_W, (a + 1) * GRID_W)
                    cols = slice(bp * LANES, (bp + 1) * LANES)
                    if not (v0 or v1):
                        bias_ref[h, var, rows, cols] = jnp.full((GRID_W, LANES), NEG_INF, F32)
                        continue
                    k = int(row_idx[var, a, 2 * bp + 1])
                    slab = jnp.broadcast_to(base_ref[h, k:k + 1, :], (GRID_W, LANES))
                    slab = pltpu.roll(slab, LANES - (NA_WC - 1), 1, stride=1, stride_axis=0)
                    if v0 and v1:
                        ok = col_valid
                    elif v0:
                        ok = jnp.logical_and(col_valid, low)
                    else:
                        ok = jnp.logical_and(col_valid, jnp.logical_not(low))
                    bias_ref[h, var, rows, cols] = jnp.where(ok, slab, NEG_INF)


def _na_attn_kernel(q_ref, k_ref, v_ref, base_ref, o_ref, bias_ref):
    @pl.when(pl.program_id(1) == 0)
    def _():
        _na_build_bias(base_ref, bias_ref)

    lo = _lane_halves()
    kc = k_ref[0, :CTX_LEN]
    vc = v_ref[0, :CTX_LEN]
    vc_aug = [_head_half(vc, lo, h, 1.0) for h in range(2)]

    q = q_ref[0, :CTX_LEN]
    outs = []
    for h in range(2):
        (e,) = _exp2_parts([_dot_nt(_head_half(q, lo, h), kc)])
        outs.append(_pv_sum(_dot(e.astype(BF16), vc_aug[h]), h))
    o_ref[0, :CTX_LEN] = jnp.where(lo, outs[0], outs[1]).astype(BF16)

    def block(rb, carry):
        s0 = jnp.clip(NA_QROWS * rb - NA_WR // 2, 0, NA_ROWS - NA_BROWS)
        var = jnp.where(rb == 0, 0, jnp.where(rb == NA_NBLK - 1, 2, 1))
        q0 = pl.multiple_of(CTX_LEN + rb * NA_QBLK, NA_QBLK)
        b0 = pl.multiple_of(CTX_LEN + s0 * GRID_W, GRID_W)
        q = q_ref[0, pl.ds(q0, NA_QBLK)]
        kb = k_ref[0, pl.ds(b0, NA_BKEYS)]
        vb = v_ref[0, pl.ds(b0, NA_BKEYS)]
        q2 = jnp.concatenate([_head_half(q, lo, 0), _head_half(q, lo, 1)], axis=0)
        bias = jnp.concatenate([bias_ref[0, var], bias_ref[1, var]], axis=0)
        acc, l = _online_attend(q2, [(kb, vb, bias), (kc, vc, None)], row_sums=True)
        o2 = acc * (1.0 / l)
        o_ref[0, pl.ds(q0, NA_QBLK)] = jnp.where(lo, o2[:NA_QBLK], o2[NA_QBLK:]).astype(BF16)
        return carry

    lax.fori_loop(0, NA_NBLK, block, 0, unroll=4)


def _na_base_rows(rpb):
    n_c = 2 * NA_WC - 1
    r = rpb.astype(F32) * LOG2E
    zrow = jnp.zeros((NA_HEADS, 1, n_c), F32)
    low = jnp.concatenate([zrow, r], axis=1)
    high = jnp.concatenate([r, zrow], axis=1)
    pad = jnp.zeros((NA_HEADS, NA_BASE_ROWS, GRID_W - n_c), F32)
    return jnp.concatenate([low, pad, high, pad], axis=2)


def _na_attn(qkv, base):
    b = qkv.shape[0]
    hp = NA_HEADS // 2
    return pl.pallas_call(
        _na_attn_kernel, grid=(hp, b),
        in_specs=[pl.BlockSpec((1, S_ALL, LANES), lambda hi, bi: (bi, 0, hi)),
                  pl.BlockSpec((1, S_ALL, LANES), lambda hi, bi: (bi, 0, hp + hi)),
                  pl.BlockSpec((1, S_ALL, LANES), lambda hi, bi: (bi, 0, 2 * hp + hi)),
                  pl.BlockSpec((2, NA_BASE_ROWS, LANES), lambda hi, bi: (hi, 0, 0))],
        out_specs=pl.BlockSpec((1, S_ALL, LANES), lambda hi, bi: (bi, 0, hi)),
        out_shape=jax.ShapeDtypeStruct((b, S_ALL, D_MODEL), BF16),
        scratch_shapes=[pltpu.VMEM((2, NA_VARIANTS, NA_QBLK, NA_BKEYS), F32)],
        compiler_params=_cparams(2), name="na_attn",
    )(qkv, qkv, qkv, base)


GQA_GROUP = GQA_Q_HEADS // GQA_KV_HEADS


def _gqa_attn_kernel(q_ref, k_ref, v_ref, o_ref, *, tq):
    lo = _lane_halves()
    outs = [[None, None] for _ in range(GQA_GROUP)]
    for j in range(2):
        qs = []
        for g in range(GQA_GROUP):
            slab, half = divmod(j * GQA_GROUP + g, 2)
            q = q_ref[0, :, slab * LANES:(slab + 1) * LANES]
            if half != j:
                q = jnp.concatenate([q[:, HEAD_DIM:], q[:, :HEAD_DIM]], axis=1)
            qs.append(_head_half(q, lo, j))
        chunks = _key_chunks(k_ref, v_ref, S_ALL, ATTN_KCHUNK, v_map=lambda v: _head_half(v, lo, j, 1.0))
        acc, _ = _online_attend(jnp.concatenate(qs, axis=0), chunks)
        o = _pv_sum(acc, j)
        for g in range(GQA_GROUP):
            slab, half = divmod(j * GQA_GROUP + g, 2)
            og = o[g * tq:(g + 1) * tq]
            if half != j:
                og = pltpu.roll(og, HEAD_DIM, 1)
            outs[slab][half] = og
    for slab in range(GQA_GROUP):
        o_ref[0, :, slab * LANES:(slab + 1) * LANES] = jnp.where(lo, outs[slab][0], outs[slab][1]).astype(BF16)


def _gqa_attn(qkv):
    b = qkv.shape[0]
    tq = 256
    qw = GQA_GROUP * 2 * HEAD_DIM
    n_pairs = GQA_KV_HEADS // 2
    kv0 = GQA_Q_HEADS * HEAD_DIM // LANES
    kern = functools.partial(_gqa_attn_kernel, tq=tq)
    return pl.pallas_call(
        kern, grid=(b, n_pairs, SEQ // tq),
        in_specs=[pl.BlockSpec((1, tq, qw), lambda bi, pi, qi: (bi, qi + CTX_LEN // tq, pi)),
                  pl.BlockSpec((1, S_ALL, LANES), lambda bi, pi, qi: (bi, 0, kv0 + pi)),
                  pl.BlockSpec((1, S_ALL, LANES), lambda bi, pi, qi: (bi, 0, kv0 + n_pairs + pi))],
        out_specs=pl.BlockSpec((1, tq, qw), lambda bi, pi, qi: (bi, qi, pi)),
        out_shape=jax.ShapeDtypeStruct((b, SEQ, D_MODEL), BF16),
        compiler_params=_cparams(3), name="gqa_attn",
    )(qkv, qkv, qkv)


def _conv_kernel(gb_ref, gc_ref, u_ref, w_ref, o_ref):
    p = gc_ref[0].astype(F32) * u_ref[0].astype(F32)
    n = p.shape[0]
    rows = lax.broadcasted_iota(jnp.int32, (n, 1), 0)
    prev = jnp.where(rows == 0, 0.0, pltpu.roll(p, 1, 0))
    nxt = jnp.where(rows == n - 1, 0.0, pltpu.roll(p, n - 1, 0))
    w = w_ref[...]
    z = prev * w[0:1] + p * w[1:2] + nxt * w[2:3]
    o_ref[0] = (gb_ref[0].astype(F32) * z).astype(BF16)


def _conv_mix(proj, w_conv):
    b, n, _ = proj.shape
    nb = D_MODEL // LANES
    spec = lambda off: pl.BlockSpec((1, n, LANES), lambda bi, ci: (bi, 0, off + ci))
    return pl.pallas_call(
        _conv_kernel, grid=(b, nb),
        in_specs=[spec(0), spec(nb), spec(2 * nb), pl.BlockSpec((3, LANES), lambda bi, ci: (0, ci))],
        out_specs=pl.BlockSpec((1, n, LANES), lambda bi, ci: (bi, 0, ci)),
        out_shape=jax.ShapeDtypeStruct((b, n, D_MODEL), BF16),
        compiler_params=_cparams(2), name="conv_mix",
    )(proj, proj, proj, w_conv)


def _route(logits_t, bias):
    scores = _sigmoid(logits_t)
    sel = scores + bias
    nt = sel.shape[1]
    idx = lax.broadcasted_iota(jnp.int32, (N_EXPERTS, nt), 0).astype(F32)
    best = gidx = None
    for g in range(N_GROUPS):
        r = [sel[g * GROUP_SIZE + i:g * GROUP_SIZE + i + 1] for i in range(GROUP_SIZE)]
        pair_sums = [r[i] + r[j] for i in range(GROUP_SIZE) for j in range(i + 1, GROUP_SIZE)]
        gs = functools.reduce(jnp.maximum, pair_sums)
        if g == 0:
            best, gidx = gs, jnp.zeros_like(gs)
        else:
            upd = gs > best
            best = jnp.where(upd, gs, best)
            gidx = jnp.where(upd, float(g), gidx)
    in_group = jnp.floor(idx * (1.0 / GROUP_SIZE)) == gidx
    masked = jnp.where(in_group, sel, NEG_INF)
    m1 = jnp.max(masked, axis=0, keepdims=True)
    i1 = jnp.min(jnp.where(masked == m1, idx, float(N_EXPERTS)), axis=0, keepdims=True)
    masked2 = jnp.where(idx == i1, -jnp.inf, masked)
    m2 = jnp.max(masked2, axis=0, keepdims=True)
    i2 = jnp.min(jnp.where(masked2 == m2, idx, float(N_EXPERTS)), axis=0, keepdims=True)
    w1 = jnp.sum(jnp.where(idx == i1, scores, 0.0), axis=0, keepdims=True)
    w2 = jnp.sum(jnp.where(idx == i2, scores, 0.0), axis=0, keepdims=True)
    inv = 1.0 / (w1 + w2)
    first_is_low = i1 < i2
    w_a = jnp.where(first_is_low, w1, w2) * inv
    w_b = jnp.where(first_is_low, w2, w1) * inv
    a_loc = jnp.minimum(i1, i2) - gidx * GROUP_SIZE
    b_loc = jnp.maximum(i1, i2) - gidx * GROUP_SIZE
    pair = a_loc * (2 * GROUP_SIZE - 1 - a_loc) * 0.5 + (b_loc - a_loc - 1.0)
    cls = gidx * N_PAIRS + pair
    rows = [w_a, w_b, cls] + [jnp.zeros_like(cls)] * (ROUTE_ROWS - 3)
    return jnp.concatenate(rows, axis=0)


def _out_kernel(o_ref, x_ref, mb_ref, mc_ref, g_ref, wo_ref, rw_ref, rb_ref,
                xn_ref, h_ref, rt_ref, *, tm, ctx_rows):
    row0 = pl.program_id(1) * tm
    mod = lambda k: _row_mod(mb_ref, mc_ref, k, row0, tm, ctx_rows)
    x = x_ref[0] + mod(2) * _dot(o_ref[0], wo_ref[...])
    xn_ref[0] = x
    h = _norm_mod(x, g_ref[...], mod(3), mod(4))
    hb = h.astype(BF16)
    hl = (h - hb.astype(F32)).astype(BF16)
    both = _dot_nt(rw_ref[...], hb)
    logits_t = both[:N_EXPERTS] + both[N_EXPERTS:] + _dot_nt(rw_ref[:N_EXPERTS], hl)
    route = _route(logits_t, rb_ref[...])
    rt_ref[0] = route
    h_ref[0] = h


def _out_proj(o, x, mods, g, w_o, rw_split, rbias, *, tm, ctx_rows, x_row_off=0):
    b, s, d = o.shape
    assert s % tm == 0 and x_row_off % tm == 0
    xo = x_row_off // tm
    kern = functools.partial(_out_kernel, tm=tm, ctx_rows=ctx_rows)
    full = lambda shape: pl.BlockSpec(shape, lambda bi, si: (0,) * len(shape))
    return pl.pallas_call(
        kern, grid=(b, s // tm),
        in_specs=[pl.BlockSpec((1, tm, d), lambda bi, si: (bi, si, 0)),
                  pl.BlockSpec((1, tm, d), lambda bi, si: (bi, si + xo, 0)),
                  pl.BlockSpec((1, 6, d), lambda bi, si: (bi, 0, 0)),
                  pl.BlockSpec((1, 6, d), lambda bi, si: (CTX_MOD_ROW, 0, 0)),
                  full((1, d)), full((d, d)), full((2 * N_EXPERTS, d)), full((N_EXPERTS, 1))],
        out_specs=[pl.BlockSpec((1, tm, d), lambda bi, si: (bi, si, 0)),
                   pl.BlockSpec((1, tm, H_EXT), lambda bi, si: (bi, si, 0)),
                   pl.BlockSpec((1, ROUTE_ROWS, tm), lambda bi, si: (bi, 0, si))],
        out_shape=[jax.ShapeDtypeStruct((b, s, d), F32),
                   jax.ShapeDtypeStruct((b, s, H_EXT), F32),
                   jax.ShapeDtypeStruct((b, ROUTE_ROWS, s), F32)],
        compiler_params=_cparams(2), name="out_proj",
    )(o, x, mods, mods, g.reshape(1, d), w_o, rw_split, rbias)


def _moe_kernel(pos_ref, ea_ref, eb_ref, nact_ref, h_hbm, zeros_hbm, rw_ref, wga_ref, wua_ref, wda_ref, wgb_ref, wub_ref,
                wdb_ref, o_ref, perm, buf, sem, zsem, *, tm, n_tok):
    i = pl.program_id(0)
    n_act = nact_ref[0]

    @pl.when(i == 0)
    def _():
        clear = pltpu.make_async_copy(zeros_hbm, perm, zsem.at[0])
        clear.start()
        clear.wait()

        def inv(t, carry):
            perm[pos_ref[t]] = t
            return carry

        lax.fori_loop(0, n_tok, inv, 0, unroll=8)
        _start_rows_loop(lambda j: perm[j], h_hbm, buf, sem, 0, tm)

    @pl.when(i < n_act)
    def _():
        slot = i % 2
        _wait_rows(h_hbm, buf, sem, slot, tm)
        @pl.when(i + 1 < n_act)
        def _():
            nxt = (i + 1) * tm
            _start_rows_loop(lambda j: perm[nxt + j], h_hbm, buf, sem, 1 - slot, tm)

        h = buf[slot]
        hb = h.astype(BF16)
        s_a = _sigmoid(jnp.sum(h * rw_ref[pl.ds(ea_ref[i], 1), :], axis=-1, keepdims=True))
        s_b = _sigmoid(jnp.sum(h * rw_ref[pl.ds(eb_ref[i], 1), :], axis=-1, keepdims=True))
        inv = 1.0 / (s_a + s_b)
        gates = (s_a * inv, s_b * inv)
        acc = None
        for lane, (wg_ref, wu_ref, wd_ref) in enumerate(((wga_ref, wua_ref, wda_ref), (wgb_ref, wub_ref, wdb_ref))):
            g = _dot(hb, wg_ref[...].astype(BF16))
            u = _dot(hb, wu_ref[...].astype(BF16))
            a = (g * _sigmoid(g)) * u * gates[lane]
            y = _dot(a.astype(BF16), wd_ref[...].astype(BF16))
            acc = y if acc is None else acc + y
        o_ref[...] = acc

    @pl.when(i >= n_act)
    def _():
        o_ref[...] = jnp.zeros_like(o_ref)


def _moe(h_ext, pos, tile_ea, tile_eb, n_active, layer, rw_t, wg, wu, wd):
    tm = MOE_TM
    n_tok = pos.shape[0]
    n_tiles = tile_ea.shape[0]
    d = D_MODEL
    wa = lambda shape: pl.BlockSpec((None, None) + shape, lambda i, ps, ea, eb, na: (layer, ea[i], 0, 0))
    wb = lambda shape: pl.BlockSpec((None, None) + shape, lambda i, ps, ea, eb, na: (layer, eb[i], 0, 0))
    kern = functools.partial(_moe_kernel, tm=tm, n_tok=n_tok)
    return pl.pallas_call(
        kern,
        grid_spec=pltpu.PrefetchScalarGridSpec(
            num_scalar_prefetch=4, grid=(n_tiles,),
            in_specs=[pl.BlockSpec(memory_space=pl.ANY), pl.BlockSpec(memory_space=pl.ANY),
                      pl.BlockSpec((N_EXPERTS, d), lambda i, *_: (0, 0)),
                      wa((d, EXPERT_FF)), wa((d, EXPERT_FF)), wa((EXPERT_FF, d)),
                      wb((d, EXPERT_FF)), wb((d, EXPERT_FF)), wb((EXPERT_FF, d))],
            out_specs=pl.BlockSpec((tm, d), lambda i, *_: (i, 0)),
            scratch_shapes=[pltpu.SMEM((n_tiles * tm,), jnp.int32), pltpu.VMEM((2, tm, H_EXT), F32),
                            pltpu.SemaphoreType.DMA((2,)), pltpu.SemaphoreType.DMA((1,))]),
        out_shape=jax.ShapeDtypeStruct((n_tiles * tm, d), F32),
        compiler_params=_cparams(1), name="moe",
    )(pos, tile_ea, tile_eb, n_active, h_ext, jnp.zeros((n_tiles * tm,), jnp.int32), rw_t, wg, wu, wd, wg, wu, wd)


def _moe_plan(route, tm):
    b, _, s = route.shape
    t = b * s
    n_tiles = t // tm + N_CLASSES
    cls = route[:, 2, :].reshape(t).astype(jnp.int32)
    onehot = (cls[:, None] == jnp.arange(N_CLASSES, dtype=jnp.int32)[None, :]).astype(jnp.int32)
    csum = jnp.cumsum(onehot, axis=0)
    counts = csum[-1]
    in_class_rank = jnp.sum((csum - onehot) * onehot, axis=1)
    tiles = (counts + tm - 1) // tm
    tile_end = jnp.cumsum(tiles)
    tile_start = tile_end - tiles
    pos = jnp.sum(onehot * tile_start[None, :], axis=1) * tm + in_class_rank
    tile_ids = jnp.arange(n_tiles, dtype=jnp.int32)
    tile_class = jnp.minimum(jnp.sum((tile_ids[:, None] >= tile_end[None, :]).astype(jnp.int32), axis=1), N_CLASSES - 1)
    first = np.array([GROUP_SIZE * (c // N_PAIRS) + _PAIRS[c % N_PAIRS][0] for c in range(N_CLASSES)], np.int32)
    second = np.array([GROUP_SIZE * (c // N_PAIRS) + _PAIRS[c % N_PAIRS][1] for c in range(N_CLASSES)], np.int32)
    i32 = lambda a: a.astype(jnp.int32)
    return i32(pos), i32(jnp.asarray(first)[tile_class]), i32(jnp.asarray(second)[tile_class]), i32(tile_end[-1:])


def _combine_kernel(pos_ref, x_ref, y_hbm, mb_ref, mc_ref, fg_ref, o_ref, buf, sem, *, tm, ctx_rows, tiles_per_sample, n_tiles):
    i = pl.program_id(0)
    slot = i % 2

    @pl.when(i == 0)
    def _():
        _start_rows_loop(lambda j: pos_ref[j], y_hbm, buf, sem, 0, tm)

    @pl.when(i + 1 < n_tiles)
    def _():
        nxt = (i + 1) * tm
        _start_rows_loop(lambda j: pos_ref[nxt + j], y_hbm, buf, sem, 1 - slot, tm)

    _wait_rows(y_hbm, buf, sem, slot, tm)
    row0 = (i % tiles_per_sample) * tm
    x = x_ref[...] + _row_mod(mb_ref, mc_ref, 5, row0, tm, ctx_rows) * buf[slot]
    o_ref[...] = _rms(x, fg_ref[...])


def _combine_final(x, y_sorted, pos, mods, final_g, *, tm, ctx_rows, s):
    t, d = x.shape
    tps = s // tm
    n_tiles = t // tm
    kern = functools.partial(_combine_kernel, tm=tm, ctx_rows=ctx_rows, tiles_per_sample=tps, n_tiles=n_tiles)
    return pl.pallas_call(
        kern,
        grid_spec=pltpu.PrefetchScalarGridSpec(
            num_scalar_prefetch=1, grid=(n_tiles,),
            in_specs=[pl.BlockSpec((tm, d), lambda i, pos: (i, 0)),
                      pl.BlockSpec(memory_space=pl.ANY),
                      pl.BlockSpec((1, 6, d), lambda i, pos: (i // tps, 0, 0)),
                      pl.BlockSpec((1, 6, d), lambda i, pos: (CTX_MOD_ROW, 0, 0)),
                      pl.BlockSpec((1, d), lambda i, pos: (0, 0))],
            out_specs=pl.BlockSpec((tm, d), lambda i, pos: (i, 0)),
            scratch_shapes=[pltpu.VMEM((2, tm, d), F32), pltpu.SemaphoreType.DMA((2,))]),
        out_shape=jax.ShapeDtypeStruct((t, d), F32),
        compiler_params=_cparams(1), name="combine",
    )(pos, x, y_sorted, mods, mods, final_g.reshape(1, d))


def _rope_tables():
    t = jnp.arange(SEQ)
    row = (t // GRID_W).astype(F32)
    col = (t % GRID_W).astype(F32)
    quarter = HEAD_DIM // 4
    inv_freq = ROPE_THETA ** (-jnp.arange(quarter, dtype=F32) / quarter)
    ang = jnp.concatenate([row[:, None] * inv_freq, col[:, None] * inv_freq], axis=-1)
    cos, sin = jnp.cos(ang), jnp.sin(ang)
    reps = LANES // HEAD_DIM
    cos_l = jnp.tile(jnp.concatenate([cos, cos], axis=-1), (1, reps))
    sin_l = jnp.tile(jnp.concatenate([-sin, sin], axis=-1), (1, reps))
    cos_all = jnp.concatenate([jnp.ones((CTX_LEN, LANES), F32), cos_l], axis=0)
    sin_all = jnp.concatenate([jnp.zeros((CTX_LEN, LANES), F32), sin_l], axis=0)
    return cos_all, sin_all


def kernel(x, c, ctx, c_ctx, ada_w, ada_b, norm1_g, norm2_g, final_norm_g, diff_w_qkv, diff_w_o, diff_lambda_q1, diff_lambda_k1, diff_lambda_q2, diff_lambda_k2, diff_subln_g, na_w_qkv, na_w_o, na_rpb, gqa_w_qkv, gqa_w_o, gqa_q_norm_g, gqa_k_norm_g, conv_w_in, conv_w, conv_w_out, router_w, router_bias, moe_w_gate, moe_w_up, moe_w_down):
    assert DEPTH % N_MIXERS == 0
    b = x.shape[0]
    cc = jnp.zeros((MOD_ROWS, D_MODEL), F32).at[:b].set(c).at[CTX_MOD_ROW].set(c_ctx)
    mods_all = _ada(cc, ada_w, ada_b).reshape(DEPTH, MOD_ROWS, 6, D_MODEL)
    rope = _rope_tables()
    rw_t = router_w.T
    rw_hi = rw_t.astype(BF16)
    rw_split = jnp.concatenate([rw_hi, (rw_t - rw_hi.astype(F32)).astype(BF16)], axis=0)
    rbias = router_bias.reshape(N_EXPERTS, 1).astype(F32)
    wg_all, wu_all, wd_all = moe_w_gate, moe_w_up, moe_w_down

    xs = jnp.concatenate([ctx, x], axis=1)
    moe_pending = None
    for i in range(DEPTH):
        m, j = i % N_MIXERS, i // N_MIXERS
        mods = mods_all[i]
        has_ctx = xs.shape[1] == S_ALL
        ctx_rows = CTX_LEN if has_ctx else 0
        ctx_advances = any(k % N_MIXERS != 3 for k in range(i + 1, DEPTH))
        common = dict(tm=768 if has_ctx else 512, ctx_rows=ctx_rows, moe=moe_pending)
        if m == 0:
            lam_init = 0.8 - 0.6 * math.exp(-0.3 * i)
            qkv, xs = _proj(xs, mods, norm1_g[i], diff_w_qkv[j].astype(BF16), q_cols=D_MODEL,
                            rope=rope, rope_cols=2 * D_MODEL, **common)
            lam_vecs = jnp.stack([diff_lambda_q1[j], diff_lambda_k1[j], diff_lambda_q2[j], diff_lambda_k2[j]])
            o = _diff_attn(qkv, lam_vecs, diff_subln_g[j], lam_init)
            w_o = diff_w_o[j]
        elif m == 1:
            qkv, xs = _proj(xs, mods, norm1_g[i], na_w_qkv[j].astype(BF16), q_cols=D_MODEL, **common)
            o = _na_attn(qkv, _na_base_rows(na_rpb[j]))
            w_o = na_w_o[j]
        elif m == 2:
            qk_cols = (GQA_Q_HEADS + GQA_KV_HEADS) * HEAD_DIM
            gain = jnp.concatenate([jnp.tile(gqa_q_norm_g[j] * QK_SCALE, GQA_Q_HEADS),
                                    jnp.tile(gqa_k_norm_g[j], GQA_KV_HEADS)]).reshape(1, qk_cols)
            qkv, xs = _proj(xs, mods, norm1_g[i], gqa_w_qkv[j].astype(BF16),
                            rope=rope, rope_cols=qk_cols, head_gain=gain, norm_cols=qk_cols, **common)
            o = _gqa_attn(qkv)
            w_o = gqa_w_o[j]
        else:
            proj, xs = _proj(xs, mods, norm1_g[i], conv_w_in[j].astype(BF16), **common)
            o = _conv_mix(proj, conv_w[j])
            w_o = conv_w_out[j]

        if has_ctx and not ctx_advances:
            tm_out, x_off, out_ctx_rows = CTX_LEN, CTX_LEN, 0
            if o.shape[1] == S_ALL:
                o = o[:, CTX_LEN:]
        elif has_ctx:
            tm_out, x_off, out_ctx_rows = 768, 0, CTX_LEN
        else:
            tm_out, x_off, out_ctx_rows = 1024, 0, 0
        xs, h_ext, route = _out_proj(o, xs, mods, norm2_g[i], w_o.astype(BF16), rw_split, rbias,
                                     tm=tm_out, ctx_rows=out_ctx_rows, x_row_off=x_off)
        s = xs.shape[1]
        pos, tile_ea, tile_eb, n_active = _moe_plan(route, MOE_TM)
        y_sorted = _moe(h_ext.reshape(b * s, H_EXT), pos, tile_ea, tile_eb, n_active, i, rw_t, wg_all, wu_all, wd_all)
        moe_pending = (y_sorted, pos, mods)
    y_sorted, pos, mods = moe_pending
    s = xs.shape[1]
    return _combine_final(xs.reshape(b * s, D_MODEL), y_sorted, pos, mods, final_norm_g,
                          tm=1024, ctx_rows=0, s=s).reshape(b, s, D_MODEL)
```

```python
import functools
import math

import numpy as np
import jax
import jax.numpy as jnp
from jax import lax
from jax.experimental import pallas as pl
from jax.experimental.pallas import tpu as pltpu

D_MODEL = 1024
BATCH = 8
SEQ = 2048
DEPTH = 4
GRID_W = 64
CTX_LEN = 256
S_ALL = CTX_LEN + SEQ
N_MIXERS = 4
HEAD_DIM = 64
ATTN_SCALE = HEAD_DIM ** -0.5
LOG2E = math.log2(math.e)
QK_SCALE = ATTN_SCALE * LOG2E
DIFF_HEADS = D_MODEL // (2 * HEAD_DIM)
NA_HEADS = D_MODEL // HEAD_DIM
NA_WR = 8
NA_WC = 16
GQA_Q_HEADS = D_MODEL // HEAD_DIM
GQA_KV_HEADS = 4
ROPE_THETA = 10000.0
N_EXPERTS = 16
N_GROUPS = 4
GROUP_SIZE = N_EXPERTS // N_GROUPS
EXPERT_FF = 512
NORM_EPS = 1e-6
NEG_INF = -1e30

LANES = 128
SUBLANES = 8
MOD_ROWS = 16
CTX_MOD_ROW = BATCH
H_EXT = D_MODEL + LANES
ROUTE_ROWS = SUBLANES
MOE_TM = 512
MAX_DMA_BURST = 64
ATTN_KCHUNK = 768
DIFF_TQ = 512
F32 = jnp.float32
BF16 = jnp.bfloat16
VMEM_LIMIT = 56 * 1024 * 1024

_NT = (((1,), (1,)), ((), ()))


def _cparams(n_axes):
    return pltpu.CompilerParams(dimension_semantics=("arbitrary",) * n_axes,
                                vmem_limit_bytes=VMEM_LIMIT)


def _dot(a, b):
    return jnp.dot(a, b, preferred_element_type=F32)


def _dot_nt(a, b):
    return lax.dot_general(a, b, _NT, preferred_element_type=F32)


def _sigmoid(x):
    return 1.0 / (1.0 + jnp.exp(-x))


def _row_mod(mb_ref, mc_ref, k, row0, tm, ctx_rows):
    mb = mb_ref[0, k:k + 1, :]
    if ctx_rows == 0:
        return mb
    rows = row0 + lax.broadcasted_iota(jnp.int32, (tm, 1), 0)
    return jnp.where(rows < ctx_rows, mc_ref[0, k:k + 1, :], mb)


def _rms(x, g):
    return x * lax.rsqrt(jnp.mean(x * x, axis=-1, keepdims=True) + NORM_EPS) * g


def _norm_mod(x, g, shift, scale):
    return _rms(x, g) * (1.0 + scale) + shift


def _row_copy(src_hbm, buf, sem, slot, src_row, dst_row):
    return pltpu.make_async_copy(src_hbm.at[pl.ds(src_row, 1)], buf.at[slot, pl.ds(dst_row, 1)], sem.at[slot])


def _wait_rows(src_hbm, buf, sem, slot, n):
    pltpu.make_async_copy(src_hbm.at[pl.ds(0, n)], buf.at[slot], sem.at[slot]).wait()


def _start_rows_loop(row_of, src_hbm, buf, sem, slot, n):
    def body(j, carry):
        _row_copy(src_hbm, buf, sem, slot, row_of(j), j).start()
        return carry

    lax.fori_loop(0, n, body, 0, unroll=8)


def _split_rows(n, parts):
    edges = [round(k * n / parts) for k in range(parts + 1)]
    return list(zip(edges[:-1], edges[1:]))


class _RowPrefetch:
    def __init__(self, row_of, src_hbm, buf, sem, slot, n, parts):
        self.args = (row_of, src_hbm, buf, sem, slot)
        self.pieces = _split_rows(n, parts)
        self.k = 0

    def issue_piece(self):
        row_of, src_hbm, buf, sem, slot = self.args
        lo, hi = self.pieces[self.k]
        self.k += 1
        for j in range(lo, hi):
            _row_copy(src_hbm, buf, sem, slot, row_of(j), j).start()


def _ada_kernel(cc_ref, w_ref, b_ref, o_ref):
    cc = cc_ref[...]
    s = (cc * _sigmoid(cc)).astype(BF16)
    o_ref[0] = _dot(s, w_ref[0].astype(BF16)) + b_ref[0]


def _ada(cc, ada_w, ada_b):
    tn = 1536
    n = 6 * D_MODEL
    return pl.pallas_call(
        _ada_kernel,
        grid=(DEPTH, n // tn),
        in_specs=[pl.BlockSpec((MOD_ROWS, D_MODEL), lambda l, j: (0, 0)),
                  pl.BlockSpec((1, D_MODEL, tn), lambda l, j: (l, 0, j)),
                  pl.BlockSpec((1, 1, tn), lambda l, j: (l, 0, j))],
        out_specs=pl.BlockSpec((1, MOD_ROWS, tn), lambda l, j: (l, 0, j)),
        out_shape=jax.ShapeDtypeStruct((DEPTH, MOD_ROWS, n), F32),
        compiler_params=_cparams(2), name="ada",
    )(cc, ada_w, ada_b.reshape(DEPTH, 1, n))


def _rope(x, cos, sin_signed):
    lane = lax.broadcasted_iota(jnp.int32, (1, LANES), 1)
    first_half = (lane % HEAD_DIM) < (HEAD_DIM // 2)
    partner = jnp.where(first_half, pltpu.roll(x, LANES - HEAD_DIM // 2, 1), pltpu.roll(x, HEAD_DIM // 2, 1))
    return x * cos + partner * sin_signed


def _proj_kernel(*refs, tm, tn, nout, ctx_rows, q_cols, rope_cols, norm_cols, fuse_moe, n_s, n_tiles):
    refs = list(refs)
    pos_ref = refs.pop(0) if fuse_moe else None
    x_ref, mb_ref, mc_ref, g_ref, w_ref = refs[:5]
    rest = refs[5:]
    cos_ref = sin_ref = hg_ref = seg_ref = y_hbm = pmb_ref = pmc_ref = None
    if rope_cols:
        cos_ref, sin_ref = rest[:2]
        rest = rest[2:]
    if norm_cols:
        hg_ref, seg_ref = rest[:2]
        rest = rest[2:]
    if fuse_moe:
        y_hbm, pmb_ref, pmc_ref = rest[:3]
        o_ref, xn_ref, buf, sem = rest[3:]
    else:
        (o_ref,) = rest

    row0 = pl.program_id(1) * tm
    x = x_ref[0]
    prefetch = None
    n_chunks = nout // tn
    pieces_per_chunk = -(-tm // (n_chunks * MAX_DMA_BURST))
    if fuse_moe:
        lin = pl.program_id(0) * n_s + pl.program_id(1)
        slot = lin % 2

        @pl.when(lin == 0)
        def _():
            _start_rows_loop(lambda j: pos_ref[j], y_hbm, buf, sem, 0, tm)

        _wait_rows(y_hbm, buf, sem, slot, tm)
        x = x + _row_mod(pmb_ref, pmc_ref, 5, row0, tm, ctx_rows) * buf[slot]
        xn_ref[0] = x
        nxt = jnp.minimum(lin + 1, n_tiles - 1) * tm
        prefetch = _RowPrefetch(lambda j: pos_ref[nxt + j], y_hbm, buf, sem, 1 - slot, tm, n_chunks * pieces_per_chunk)

    shift = _row_mod(mb_ref, mc_ref, 0, row0, tm, ctx_rows)
    scale = _row_mod(mb_ref, mc_ref, 1, row0, tm, ctx_rows)
    hb = _norm_mod(x, g_ref[...], shift, scale).astype(BF16)
    for c in range(n_chunks):
        lo = c * tn
        acc = _dot(hb, w_ref[:, lo:lo + tn])
        if prefetch is not None:
            prefetch.issue_piece()
        if lo < q_cols:
            acc = acc * QK_SCALE
        if lo < norm_cols:
            sq = acc * acc
            sq_hi = sq.astype(BF16)
            sq_lo = (sq - sq_hi.astype(F32)).astype(BF16)
            ss = _dot(sq_hi, seg_ref[...]) + _dot(sq_lo, seg_ref[...])
            acc = acc * lax.rsqrt(ss * (1.0 / HEAD_DIM) + NORM_EPS) * hg_ref[:, lo:lo + tn]
        if lo < rope_cols:
            cos, sin = cos_ref[...], sin_ref[...]
            acc = jnp.concatenate(
                [_rope(acc[:, k * LANES:(k + 1) * LANES], cos, sin) for k in range(tn // LANES)], axis=1)
        o_ref[0, :, lo:lo + tn] = acc.astype(BF16)
        if prefetch is not None:
            for _ in range(pieces_per_chunk - 1):
                prefetch.issue_piece()

    if fuse_moe:
        @pl.when(lin == n_tiles - 1)
        def _():
            _wait_rows(y_hbm, buf, sem, 1 - slot, tm)


def _proj(x, mods, g, w, *, tm, ctx_rows, q_cols=0, rope=None, rope_cols=0, head_gain=None, norm_cols=0, moe=None):
    b, s, d = x.shape
    nout = w.shape[1]
    tn = 256
    n_s = s // tm
    fuse = moe is not None
    assert s % tm == 0 and nout % tn == 0 and rope_cols % tn == 0 and norm_cols % tn == 0 and q_cols % tn == 0
    im = lambda f: (lambda bi, si, *_: f(bi, si))
    in_specs = [pl.BlockSpec((1, tm, d), im(lambda bi, si: (bi, si, 0))),
                pl.BlockSpec((1, 6, d), im(lambda bi, si: (bi, 0, 0))),
                pl.BlockSpec((1, 6, d), im(lambda bi, si: (CTX_MOD_ROW, 0, 0))),
                pl.BlockSpec((1, d), im(lambda bi, si: (0, 0))),
                pl.BlockSpec((d, nout), im(lambda bi, si: (0, 0)))]
    args = [x, mods, mods, g.reshape(1, d), w]
    if rope_cols:
        in_specs += [pl.BlockSpec((tm, LANES), im(lambda bi, si: (si, 0)))] * 2
        args += list(rope)
    if norm_cols:
        seg = np.kron(np.eye(tn // HEAD_DIM), np.ones((HEAD_DIM, HEAD_DIM))).astype(np.float32)
        in_specs += [pl.BlockSpec((1, norm_cols), im(lambda bi, si: (0, 0))),
                     pl.BlockSpec((tn, tn), im(lambda bi, si: (0, 0)))]
        args += [head_gain, jnp.asarray(seg, BF16)]
    out_specs = [pl.BlockSpec((1, tm, nout), im(lambda bi, si: (bi, si, 0)))]
    out_shape = [jax.ShapeDtypeStruct((b, s, nout), BF16)]
    scratch = []
    n_prefetch = 0
    if fuse:
        y_sorted, pos, prev_mods = moe
        in_specs += [pl.BlockSpec(memory_space=pl.ANY),
                     pl.BlockSpec((1, 6, d), im(lambda bi, si: (bi, 0, 0))),
                     pl.BlockSpec((1, 6, d), im(lambda bi, si: (CTX_MOD_ROW, 0, 0)))]
        args = [pos] + args + [y_sorted, prev_mods, prev_mods]
        out_specs.append(pl.BlockSpec((1, tm, d), im(lambda bi, si: (bi, si, 0))))
        out_shape.append(jax.ShapeDtypeStruct((b, s, d), F32))
        scratch = [pltpu.VMEM((2, tm, d), F32), pltpu.SemaphoreType.DMA((2,))]
        n_prefetch = 1
    kern = functools.partial(_proj_kernel, tm=tm, tn=tn, nout=nout, ctx_rows=ctx_rows, q_cols=q_cols,
                             rope_cols=rope_cols, norm_cols=norm_cols, fuse_moe=fuse, n_s=n_s, n_tiles=b * n_s)
    res = pl.pallas_call(
        kern,
        grid_spec=pltpu.PrefetchScalarGridSpec(num_scalar_prefetch=n_prefetch, grid=(b, n_s), in_specs=in_specs,
                                               out_specs=out_specs, scratch_shapes=scratch),
        out_shape=out_shape, compiler_params=_cparams(2), name="proj",
    )(*args)
    return (res[0], res[1]) if fuse else (res[0], x)


def _lane_halves():
    lane = lax.broadcasted_iota(jnp.int32, (1, LANES), 1)
    return lane < HEAD_DIM


def _head_half(x, lo, half, fill=0.0):
    f = jnp.full_like(x, fill)
    return jnp.where(lo, x, f) if half == 0 else jnp.where(lo, f, x)


def _exp2_parts(s_list):
    m = functools.reduce(jnp.maximum, [jnp.max(s, axis=-1, keepdims=True) for s in s_list])
    return [jnp.exp2(s - m) for s in s_list]


def _key_chunks(k_ref, v_ref, nk, chunk, v_map=lambda v: v):
    return [(k_ref[0, c0:min(c0 + chunk, nk)], v_map(v_ref[0, c0:min(c0 + chunk, nk)]), None)
            for c0 in range(0, nk, chunk)]


def _online_attend(q, chunks, row_sums=False):
    m = acc = l = None
    for k, v, bias in chunks:
        s = _dot_nt(q, k)
        if bias is not None:
            s = s + bias
        mc = jnp.max(s, axis=-1, keepdims=True)
        if acc is None:
            m = mc
            e = jnp.exp2(s - m)
            acc = _dot(e.astype(BF16), v)
            if row_sums:
                l = jnp.sum(e, axis=-1, keepdims=True)
        else:
            m_new = jnp.maximum(m, mc)
            alpha = jnp.exp2(m - m_new)
            e = jnp.exp2(s - m_new)
            acc = acc * alpha + _dot(e.astype(BF16), v)
            if row_sums:
                l = l * alpha + jnp.sum(e, axis=-1, keepdims=True)
            m = m_new
    return acc, l


def _pv_sum(o, half):
    l = o[:, HEAD_DIM:HEAD_DIM + 1] if half == 0 else o[:, 0:1]
    return o * (1.0 / l)


def _diff_attn_kernel(lam_ref, q_ref, k_ref, v_ref, g_ref, o_ref, *, lam_init, tq):
    lv = lam_ref[...]
    lam = (jnp.exp(jnp.sum(lv[0:1] * lv[1:2], axis=-1, keepdims=True))
           - jnp.exp(jnp.sum(lv[2:3] * lv[3:4], axis=-1, keepdims=True)) + lam_init)
    lo = _lane_halves()

    def attend(q, nk):
        nq = q.shape[0]
        q2 = jnp.concatenate([_head_half(q, lo, 0), _head_half(q, lo, 1)], axis=0)
        acc, l = _online_attend(q2, _key_chunks(k_ref, v_ref, nk, ATTN_KCHUNK), row_sums=True)
        o = acc[:nq] * (1.0 / l[:nq]) - acc[nq:] * (lam / l[nq:])
        return (_rms(o, g_ref[...]) * (1.0 - lam_init)).astype(BF16)

    o_ref[0, :CTX_LEN] = attend(q_ref[0, :CTX_LEN], CTX_LEN)

    def tile(t, carry):
        r0 = pl.multiple_of(CTX_LEN + t * tq, math.gcd(CTX_LEN, tq))
        o_ref[0, pl.ds(r0, tq)] = attend(q_ref[0, pl.ds(r0, tq)], S_ALL)
        return carry

    lax.fori_loop(0, SEQ // tq, tile, 0, unroll=4)


def _diff_attn(qkv, lam_vecs, subln_g, lam_init):
    b = qkv.shape[0]
    h = DIFF_HEADS
    kern = functools.partial(_diff_attn_kernel, lam_init=lam_init, tq=DIFF_TQ)
    slab = lambda off: pl.BlockSpec((1, S_ALL, LANES), lambda bi, hi: (bi, 0, off + hi))
    return pl.pallas_call(
        kern, grid=(b, h),
        in_specs=[pl.BlockSpec((4, HEAD_DIM), lambda bi, hi: (0, 0)), slab(0), slab(h), slab(2 * h),
                  pl.BlockSpec((1, LANES), lambda bi, hi: (0, 0))],
        out_specs=slab(0),
        out_shape=jax.ShapeDtypeStruct((b, S_ALL, D_MODEL), BF16),
        compiler_params=_cparams(2), name="diff_attn",
    )(lam_vecs, qkv, qkv, qkv, subln_g.reshape(1, LANES))


NA_ROWS = SEQ // GRID_W
NA_QROWS = 4
NA_QBLK = NA_QROWS * GRID_W
NA_BROWS = NA_QROWS + NA_WR
NA_BKEYS = NA_BROWS * GRID_W
NA_NBLK = NA_ROWS // NA_QROWS
NA_VARIANTS = 3
NA_BASE_ROWS = 2 * NA_WR


def _na_row_geometry():
    row_idx = np.zeros((NA_VARIANTS, NA_QROWS, NA_BROWS), np.int32)
    row_valid = np.zeros((NA_VARIANTS, NA_QROWS, NA_BROWS), bool)
    for var, rb in enumerate((0, 1, NA_NBLK - 1)):
        s0 = np.clip(NA_QROWS * rb - NA_WR // 2, 0, NA_ROWS - NA_BROWS)
        for a in range(NA_QROWS):
            r = NA_QROWS * rb + a
            rs = np.clip(r - NA_WR // 2, 0, NA_ROWS - NA_WR)
            for bb in range(NA_BROWS):
                kr = s0 + bb
                row_valid[var, a, bb] = rs <= kr < rs + NA_WR
                row_idx[var, a, bb] = kr - r + NA_WR - 1
    return row_idx, row_valid


def _na_build_bias(base_ref, bias_ref):
    row_idx, row_valid = _na_row_geometry()
    qc = lax.broadcasted_iota(jnp.int32, (GRID_W, LANES), 0)
    lane = lax.broadcasted_iota(jnp.int32, (GRID_W, LANES), 1)
    kc = lane % GRID_W
    win0 = jnp.clip(qc - NA_WC // 2, 0, GRID_W - NA_WC)
    col_valid = (kc >= win0) & (kc < win0 + NA_WC)
    low = lane < GRID_W
    for h in range(2):
        for var in range(NA_VARIANTS):
            for a in range(NA_QROWS):
                for bp in range(NA_BROWS // 2):
                    v0, v1 = row_valid[var, a, 2 * bp], row_valid[var, a, 2 * bp + 1]
                    rows = slice(a * GRID_W, (a + 1) * GRID_W)
                    cols = slice(bp * LANES, (bp + 1) * LANES)
                    if not (v0 or v1):
                        bias_ref[h, var, rows, cols] = jnp.full((GRID_W, LANES), NEG_INF, F32)
                        continue
                    k = int(row_idx[var, a, 2 * bp + 1])
                    slab = jnp.broadcast_to(base_ref[h, k:k + 1, :], (GRID_W, LANES))
                    slab = pltpu.roll(slab, LANES - (NA_WC - 1), 1, stride=1, stride_axis=0)
                    if v0 and v1:
                        ok = col_valid
                    elif v0:
                        ok = jnp.logical_and(col_valid, low)
                    else:
                        ok = jnp.logical_and(col_valid, jnp.logical_not(low))
                    bias_ref[h, var, rows, cols] = jnp.where(ok, slab, NEG_INF)


def _na_attn_kernel(q_ref, k_ref, v_ref, base_ref, o_ref, bias_ref):
    @pl.when(pl.program_id(1) == 0)
    def _():
        _na_build_bias(base_ref, bias_ref)

    lo = _lane_halves()
    kc = k_ref[0, :CTX_LEN]
    vc = v_ref[0, :CTX_LEN]
    vc_aug = [_head_half(vc, lo, h, 1.0) for h in range(2)]

    q = q_ref[0, :CTX_LEN]
    outs = []
    for h in range(2):
        (e,) = _exp2_parts([_dot_nt(_head_half(q, lo, h), kc)])
        outs.append(_pv_sum(_dot(e.astype(BF16), vc_aug[h]), h))
    o_ref[0, :CTX_LEN] = jnp.where(lo, outs[0], outs[1]).astype(BF16)

    def block(rb, carry):
        s0 = jnp.clip(NA_QROWS * rb - NA_WR // 2, 0, NA_ROWS - NA_BROWS)
        var = jnp.where(rb == 0, 0, jnp.where(rb == NA_NBLK - 1, 2, 1))
        q0 = pl.multiple_of(CTX_LEN + rb * NA_QBLK, NA_QBLK)
        b0 = pl.multiple_of(CTX_LEN + s0 * GRID_W, GRID_W)
        q = q_ref[0, pl.ds(q0, NA_QBLK)]
        kb = k_ref[0, pl.ds(b0, NA_BKEYS)]
        vb = v_ref[0, pl.ds(b0, NA_BKEYS)]
        q2 = jnp.concatenate([_head_half(q, lo, 0), _head_half(q, lo, 1)], axis=0)
        bias = jnp.concatenate([bias_ref[0, var], bias_ref[1, var]], axis=0)
        acc, l = _online_attend(q2, [(kb, vb, bias), (kc, vc, None)], row_sums=True)
        o2 = acc * (1.0 / l)
        o_ref[0, pl.ds(q0, NA_QBLK)] = jnp.where(lo, o2[:NA_QBLK], o2[NA_QBLK:]).astype(BF16)
        return carry

    lax.fori_loop(0, NA_NBLK, block, 0, unroll=4)


def _na_base_rows(rpb):
    n_c = 2 * NA_WC - 1
    r = rpb.astype(F32) * LOG2E
    zrow = jnp.zeros((NA_HEADS, 1, n_c), F32)
    low = jnp.concatenate([zrow, r], axis=1)
    high = jnp.concatenate([r, zrow], axis=1)
    pad = jnp.zeros((NA_HEADS, NA_BASE_ROWS, GRID_W - n_c), F32)
    return jnp.concatenate([low, pad, high, pad], axis=2)


def _na_attn(qkv, base):
    b = qkv.shape[0]
    hp = NA_HEADS // 2
    return pl.pallas_call(
        _na_attn_kernel, grid=(hp, b),
        in_specs=[pl.BlockSpec((1, S_ALL, LANES), lambda hi, bi: (bi, 0, hi)),
                  pl.BlockSpec((1, S_ALL, LANES), lambda hi, bi: (bi, 0, hp + hi)),
                  pl.BlockSpec((1, S_ALL, LANES), lambda hi, bi: (bi, 0, 2 * hp + hi)),
                  pl.BlockSpec((2, NA_BASE_ROWS, LANES), lambda hi, bi: (hi, 0, 0))],
        out_specs=pl.BlockSpec((1, S_ALL, LANES), lambda hi, bi: (bi, 0, hi)),
        out_shape=jax.ShapeDtypeStruct((b, S_ALL, D_MODEL), BF16),
        scratch_shapes=[pltpu.VMEM((2, NA_VARIANTS, NA_QBLK, NA_BKEYS), F32)],
        compiler_params=_cparams(2), name="na_attn",
    )(qkv, qkv, qkv, base)


GQA_GROUP = GQA_Q_HEADS // GQA_KV_HEADS


def _gqa_attn_kernel(q_ref, k_ref, v_ref, o_ref, *, tq):
    lo = _lane_halves()
    outs = [[None, None] for _ in range(GQA_GROUP)]
    for j in range(2):
        qs = []
        for g in range(GQA_GROUP):
            slab, half = divmod(j * GQA_GROUP + g, 2)
            q = q_ref[0, :, slab * LANES:(slab + 1) * LANES]
            if half != j:
                q = jnp.concatenate([q[:, HEAD_DIM:], q[:, :HEAD_DIM]], axis=1)
            qs.append(_head_half(q, lo, j))
        chunks = _key_chunks(k_ref, v_ref, S_ALL, ATTN_KCHUNK, v_map=lambda v: _head_half(v, lo, j, 1.0))
        acc, _ = _online_attend(jnp.concatenate(qs, axis=0), chunks)
        o = _pv_sum(acc, j)
        for g in range(GQA_GROUP):
            slab, half = divmod(j * GQA_GROUP + g, 2)
            og = o[g * tq:(g + 1) * tq]
            if half != j:
                og = pltpu.roll(og, HEAD_DIM, 1)
            outs[slab][half] = og
    for slab in range(GQA_GROUP):
        o_ref[0, :, slab * LANES:(slab + 1) * LANES] = jnp.where(lo, outs[slab][0], outs[slab][1]).astype(BF16)


def _gqa_attn(qkv):
    b = qkv.shape[0]
    tq = 256
    qw = GQA_GROUP * 2 * HEAD_DIM
    n_pairs = GQA_KV_HEADS // 2
    kv0 = GQA_Q_HEADS * HEAD_DIM // LANES
    kern = functools.partial(_gqa_attn_kernel, tq=tq)
    return pl.pallas_call(
        kern, grid=(b, n_pairs, SEQ // tq),
        in_specs=[pl.BlockSpec((1, tq, qw), lambda bi, pi, qi: (bi, qi + CTX_LEN // tq, pi)),
                  pl.BlockSpec((1, S_ALL, LANES), lambda bi, pi, qi: (bi, 0, kv0 + pi)),
                  pl.BlockSpec((1, S_ALL, LANES), lambda bi, pi, qi: (bi, 0, kv0 + n_pairs + pi))],
        out_specs=pl.BlockSpec((1, tq, qw), lambda bi, pi, qi: (bi, qi, pi)),
        out_shape=jax.ShapeDtypeStruct((b, SEQ, D_MODEL), BF16),
        compiler_params=_cparams(3), name="gqa_attn",
    )(qkv, qkv, qkv)


def _conv_kernel(gb_ref, gc_ref, u_ref, w_ref, o_ref):
    p = gc_ref[0].astype(F32) * u_ref[0].astype(F32)
    n = p.shape[0]
    rows = lax.broadcasted_iota(jnp.int32, (n, 1), 0)
    prev = jnp.where(rows == 0, 0.0, pltpu.roll(p, 1, 0))
    nxt = jnp.where(rows == n - 1, 0.0, pltpu.roll(p, n - 1, 0))
    w = w_ref[...]
    z = prev * w[0:1] + p * w[1:2] + nxt * w[2:3]
    o_ref[0] = (gb_ref[0].astype(F32) * z).astype(BF16)


def _conv_mix(proj, w_conv):
    b, n, _ = proj.shape
    nb = D_MODEL // LANES
    spec = lambda off: pl.BlockSpec((1, n, LANES), lambda bi, ci: (bi, 0, off + ci))
    return pl.pallas_call(
        _conv_kernel, grid=(b, nb),
        in_specs=[spec(0), spec(nb), spec(2 * nb), pl.BlockSpec((3, LANES), lambda bi, ci: (0, ci))],
        out_specs=pl.BlockSpec((1, n, LANES), lambda bi, ci: (bi, 0, ci)),
        out_shape=jax.ShapeDtypeStruct((b, n, D_MODEL), BF16),
        compiler_params=_cparams(2), name="conv_mix",
    )(proj, proj, proj, w_conv)


def _route(logits_t, bias):
    scores = _sigmoid(logits_t)
    sel = scores + bias
    nt = sel.shape[1]
    idx = lax.broadcasted_iota(jnp.int32, (N_EXPERTS, nt), 0).astype(F32)
    best = gidx = None
    for g in range(N_GROUPS):
        r = [sel[g * GROUP_SIZE + i:g * GROUP_SIZE + i + 1] for i in range(GROUP_SIZE)]
        pair_sums = [r[i] + r[j] for i in range(GROUP_SIZE) for j in range(i + 1, GROUP_SIZE)]
        gs = functools.reduce(jnp.maximum, pair_sums)
        if g == 0:
            best, gidx = gs, jnp.zeros_like(gs)
        else:
            upd = gs > best
            best = jnp.where(upd, gs, best)
            gidx = jnp.where(upd, float(g), gidx)
    in_group = jnp.floor(idx * (1.0 / GROUP_SIZE)) == gidx
    masked = jnp.where(in_group, sel, NEG_INF)
    m1 = jnp.max(masked, axis=0, keepdims=True)
    i1 = jnp.min(jnp.where(masked == m1, idx, float(N_EXPERTS)), axis=0, keepdims=True)
    masked2 = jnp.where(idx == i1, -jnp.inf, masked)
    m2 = jnp.max(masked2, axis=0, keepdims=True)
    i2 = jnp.min(jnp.where(masked2 == m2, idx, float(N_EXPERTS)), axis=0, keepdims=True)
    w1 = jnp.sum(jnp.where(idx == i1, scores, 0.0), axis=0, keepdims=True)
    w2 = jnp.sum(jnp.where(idx == i2, scores, 0.0), axis=0, keepdims=True)
    inv = 1.0 / (w1 + w2)
    gates = jnp.where(idx == i1, w1 * inv, 0.0) + jnp.where(idx == i2, w2 * inv, 0.0)
    rows = []
    for i in range(GROUP_SIZE):
        rows.append(functools.reduce(
            lambda a, b: a + b,
            [jnp.where(gidx == float(g), gates[g * GROUP_SIZE + i:g * GROUP_SIZE + i + 1], 0.0) for g in range(N_GROUPS)]))
    rows.append(gidx)
    rows += [jnp.zeros_like(gidx)] * (ROUTE_ROWS - len(rows))
    return jnp.concatenate(rows, axis=0)


def _out_kernel(o_ref, x_ref, mb_ref, mc_ref, g_ref, wo_ref, rw_ref, rb_ref,
                xn_ref, h_ref, rt_ref, *, tm, ctx_rows):
    row0 = pl.program_id(1) * tm
    mod = lambda k: _row_mod(mb_ref, mc_ref, k, row0, tm, ctx_rows)
    x = x_ref[0] + mod(2) * _dot(o_ref[0], wo_ref[...])
    xn_ref[0] = x
    h = _norm_mod(x, g_ref[...], mod(3), mod(4))
    hb = h.astype(BF16)
    hl = (h - hb.astype(F32)).astype(BF16)
    both = _dot_nt(rw_ref[...], hb)
    logits_t = both[:N_EXPERTS] + both[N_EXPERTS:] + _dot_nt(rw_ref[:N_EXPERTS], hl)
    route = _route(logits_t, rb_ref[...])
    rt_ref[0] = route
    h_ref[0, :, :D_MODEL] = h
    h_ref[0, :, D_MODEL:] = jnp.concatenate([route, jnp.zeros((LANES - ROUTE_ROWS, tm), F32)], axis=0).T


def _out_proj(o, x, mods, g, w_o, rw_split, rbias, *, tm, ctx_rows, x_row_off=0):
    b, s, d = o.shape
    assert s % tm == 0 and x_row_off % tm == 0
    xo = x_row_off // tm
    kern = functools.partial(_out_kernel, tm=tm, ctx_rows=ctx_rows)
    full = lambda shape: pl.BlockSpec(shape, lambda bi, si: (0,) * len(shape))
    return pl.pallas_call(
        kern, grid=(b, s // tm),
        in_specs=[pl.BlockSpec((1, tm, d), lambda bi, si: (bi, si, 0)),
                  pl.BlockSpec((1, tm, d), lambda bi, si: (bi, si + xo, 0)),
                  pl.BlockSpec((1, 6, d), lambda bi, si: (bi, 0, 0)),
                  pl.BlockSpec((1, 6, d), lambda bi, si: (CTX_MOD_ROW, 0, 0)),
                  full((1, d)), full((d, d)), full((2 * N_EXPERTS, d)), full((N_EXPERTS, 1))],
        out_specs=[pl.BlockSpec((1, tm, d), lambda bi, si: (bi, si, 0)),
                   pl.BlockSpec((1, tm, H_EXT), lambda bi, si: (bi, si, 0)),
                   pl.BlockSpec((1, ROUTE_ROWS, tm), lambda bi, si: (bi, 0, si))],
        out_shape=[jax.ShapeDtypeStruct((b, s, d), F32),
                   jax.ShapeDtypeStruct((b, s, H_EXT), F32),
                   jax.ShapeDtypeStruct((b, ROUTE_ROWS, s), F32)],
        compiler_params=_cparams(2), name="out_proj",
    )(o, x, mods, mods, g.reshape(1, d), w_o, rw_split, rbias)


def _moe_kernel(pos_ref, tgroup_ref, nact_ref, h_hbm, zeros_hbm, wg_ref, wu_ref, wd_ref, o_ref, perm, buf, sem, zsem,
                *, tm, n_tok):
    i = pl.program_id(0)
    n_act = nact_ref[0]

    @pl.when(i == 0)
    def _():
        clear = pltpu.make_async_copy(zeros_hbm, perm, zsem.at[0])
        clear.start()
        clear.wait()

        def inv(t, carry):
            perm[pos_ref[t]] = t
            return carry

        lax.fori_loop(0, n_tok, inv, 0, unroll=8)
        _start_rows_loop(lambda j: perm[j], h_hbm, buf, sem, 0, tm)

    @pl.when(i < n_act)
    def _():
        slot = i % 2
        _wait_rows(h_hbm, buf, sem, slot, tm)
        @pl.when(i + 1 < n_act)
        def _():
            nxt = (i + 1) * tm
            _start_rows_loop(lambda j: perm[nxt + j], h_hbm, buf, sem, 1 - slot, tm)

        hb = buf[slot, :, :D_MODEL].astype(BF16)
        gates = buf[slot, :, D_MODEL:]
        acc = jnp.zeros((tm, D_MODEL), F32)
        for j in range(GROUP_SIZE):
            g = _dot(hb, wg_ref[j].astype(BF16))
            u = _dot(hb, wu_ref[j].astype(BF16))
            a = (g * _sigmoid(g)) * u * gates[:, j:j + 1]
            acc = acc + _dot(a.astype(BF16), wd_ref[j].astype(BF16))
        o_ref[...] = acc

    @pl.when(i >= n_act)
    def _():
        o_ref[...] = jnp.zeros_like(o_ref)


def _moe(h_ext, pos, tile_group, n_active, layer, wg, wu, wd):
    tm = MOE_TM
    n_tok = pos.shape[0]
    n_tiles = tile_group.shape[0]
    d = D_MODEL
    wspec = lambda shape: pl.BlockSpec((None,) + shape, lambda i, ps, tg, na: (layer, tg[i], 0, 0),
                                       pipeline_mode=pl.Buffered(1))
    kern = functools.partial(_moe_kernel, tm=tm, n_tok=n_tok)
    return pl.pallas_call(
        kern,
        grid_spec=pltpu.PrefetchScalarGridSpec(
            num_scalar_prefetch=3, grid=(n_tiles,),
            in_specs=[pl.BlockSpec(memory_space=pl.ANY), pl.BlockSpec(memory_space=pl.ANY),
                      wspec((GROUP_SIZE, d, EXPERT_FF)), wspec((GROUP_SIZE, d, EXPERT_FF)),
                      wspec((GROUP_SIZE, EXPERT_FF, d))],
            out_specs=pl.BlockSpec((tm, d), lambda i, *_: (i, 0)),
            scratch_shapes=[pltpu.SMEM((n_tiles * tm,), jnp.int32), pltpu.VMEM((2, tm, H_EXT), F32),
                            pltpu.SemaphoreType.DMA((2,)), pltpu.SemaphoreType.DMA((1,))]),
        out_shape=jax.ShapeDtypeStruct((n_tiles * tm, d), F32),
        compiler_params=_cparams(1), name="moe",
    )(pos, tile_group, n_active, h_ext, jnp.zeros((n_tiles * tm,), jnp.int32), wg, wu, wd)


def _moe_plan(route, tm):
    b, _, s = route.shape
    t = b * s
    n_tiles = t // tm + N_GROUPS
    gid = route[:, GROUP_SIZE, :].reshape(t).astype(jnp.int32)
    onehot = (gid[:, None] == jnp.arange(N_GROUPS, dtype=jnp.int32)[None, :]).astype(jnp.int32)
    csum = jnp.cumsum(onehot, axis=0)
    counts = csum[-1]
    in_group_rank = jnp.sum((csum - onehot) * onehot, axis=1)
    tiles = (counts + tm - 1) // tm
    tile_end = jnp.cumsum(tiles)
    tile_start = tile_end - tiles
    pos = jnp.sum(onehot * tile_start[None, :], axis=1) * tm + in_group_rank
    tile_ids = jnp.arange(n_tiles, dtype=jnp.int32)
    tile_group = jnp.minimum(jnp.sum((tile_ids[:, None] >= tile_end[None, :]).astype(jnp.int32), axis=1), N_GROUPS - 1)
    i32 = lambda a: a.astype(jnp.int32)
    return i32(pos), i32(tile_group), i32(tile_end[-1:])


def _combine_kernel(pos_ref, x_ref, y_hbm, mb_ref, mc_ref, fg_ref, o_ref, buf, sem, *, tm, ctx_rows, tiles_per_sample, n_tiles):
    i = pl.program_id(0)
    slot = i % 2

    @pl.when(i == 0)
    def _():
        _start_rows_loop(lambda j: pos_ref[j], y_hbm, buf, sem, 0, tm)

    @pl.when(i + 1 < n_tiles)
    def _():
        nxt = (i + 1) * tm
        _start_rows_loop(lambda j: pos_ref[nxt + j], y_hbm, buf, sem, 1 - slot, tm)

    _wait_rows(y_hbm, buf, sem, slot, tm)
    row0 = (i % tiles_per_sample) * tm
    x = x_ref[...] + _row_mod(mb_ref, mc_ref, 5, row0, tm, ctx_rows) * buf[slot]
    o_ref[...] = _rms(x, fg_ref[...])


def _combine_final(x, y_sorted, pos, mods, final_g, *, tm, ctx_rows, s):
    t, d = x.shape
    tps = s // tm
    n_tiles = t // tm
    kern = functools.partial(_combine_kernel, tm=tm, ctx_rows=ctx_rows, tiles_per_sample=tps, n_tiles=n_tiles)
    return pl.pallas_call(
        kern,
        grid_spec=pltpu.PrefetchScalarGridSpec(
            num_scalar_prefetch=1, grid=(n_tiles,),
            in_specs=[pl.BlockSpec((tm, d), lambda i, pos: (i, 0)),
                      pl.BlockSpec(memory_space=pl.ANY),
                      pl.BlockSpec((1, 6, d), lambda i, pos: (i // tps, 0, 0)),
                      pl.BlockSpec((1, 6, d), lambda i, pos: (CTX_MOD_ROW, 0, 0)),
                      pl.BlockSpec((1, d), lambda i, pos: (0, 0))],
            out_specs=pl.BlockSpec((tm, d), lambda i, pos: (i, 0)),
            scratch_shapes=[pltpu.VMEM((2, tm, d), F32), pltpu.SemaphoreType.DMA((2,))]),
        out_shape=jax.ShapeDtypeStruct((t, d), F32),
        compiler_params=_cparams(1), name="combine",
    )(pos, x, y_sorted, mods, mods, final_g.reshape(1, d))


def _rope_tables():
    t = jnp.arange(SEQ)
    row = (t // GRID_W).astype(F32)
    col = (t % GRID_W).astype(F32)
    quarter = HEAD_DIM // 4
    inv_freq = ROPE_THETA ** (-jnp.arange(quarter, dtype=F32) / quarter)
    ang = jnp.concatenate([row[:, None] * inv_freq, col[:, None] * inv_freq], axis=-1)
    cos, sin = jnp.cos(ang), jnp.sin(ang)
    reps = LANES // HEAD_DIM
    cos_l = jnp.tile(jnp.concatenate([cos, cos], axis=-1), (1, reps))
    sin_l = jnp.tile(jnp.concatenate([-sin, sin], axis=-1), (1, reps))
    cos_all = jnp.concatenate([jnp.ones((CTX_LEN, LANES), F32), cos_l], axis=0)
    sin_all = jnp.concatenate([jnp.zeros((CTX_LEN, LANES), F32), sin_l], axis=0)
    return cos_all, sin_all


def kernel(x, c, ctx, c_ctx, ada_w, ada_b, norm1_g, norm2_g, final_norm_g, diff_w_qkv, diff_w_o, diff_lambda_q1, diff_lambda_k1, diff_lambda_q2, diff_lambda_k2, diff_subln_g, na_w_qkv, na_w_o, na_rpb, gqa_w_qkv, gqa_w_o, gqa_q_norm_g, gqa_k_norm_g, conv_w_in, conv_w, conv_w_out, router_w, router_bias, moe_w_gate, moe_w_up, moe_w_down):
    assert DEPTH % N_MIXERS == 0
    b = x.shape[0]
    cc = jnp.zeros((MOD_ROWS, D_MODEL), F32).at[:b].set(c).at[CTX_MOD_ROW].set(c_ctx)
    mods_all = _ada(cc, ada_w, ada_b).reshape(DEPTH, MOD_ROWS, 6, D_MODEL)
    rope = _rope_tables()
    rw_t = router_w.T
    rw_hi = rw_t.astype(BF16)
    rw_split = jnp.concatenate([rw_hi, (rw_t - rw_hi.astype(F32)).astype(BF16)], axis=0)
    rbias = router_bias.reshape(N_EXPERTS, 1).astype(F32)
    wg_all, wu_all, wd_all = moe_w_gate, moe_w_up, moe_w_down

    xs = jnp.concatenate([ctx, x], axis=1)
    moe_pending = None
    for i in range(DEPTH):
        m, j = i % N_MIXERS, i // N_MIXERS
        mods = mods_all[i]
        has_ctx = xs.shape[1] == S_ALL
        ctx_rows = CTX_LEN if has_ctx else 0
        ctx_advances = any(k % N_MIXERS != 3 for k in range(i + 1, DEPTH))
        common = dict(tm=768 if has_ctx else 512, ctx_rows=ctx_rows, moe=moe_pending)
        if m == 0:
            lam_init = 0.8 - 0.6 * math.exp(-0.3 * i)
            qkv, xs = _proj(xs, mods, norm1_g[i], diff_w_qkv[j].astype(BF16), q_cols=D_MODEL,
                            rope=rope, rope_cols=2 * D_MODEL, **common)
            lam_vecs = jnp.stack([diff_lambda_q1[j], diff_lambda_k1[j], diff_lambda_q2[j], diff_lambda_k2[j]])
            o = _diff_attn(qkv, lam_vecs, diff_subln_g[j], lam_init)
            w_o = diff_w_o[j]
        elif m == 1:
            qkv, xs = _proj(xs, mods, norm1_g[i], na_w_qkv[j].astype(BF16), q_cols=D_MODEL, **common)
            o = _na_attn(qkv, _na_base_rows(na_rpb[j]))
            w_o = na_w_o[j]
        elif m == 2:
            qk_cols = (GQA_Q_HEADS + GQA_KV_HEADS) * HEAD_DIM
            gain = jnp.concatenate([jnp.tile(gqa_q_norm_g[j] * QK_SCALE, GQA_Q_HEADS),
                                    jnp.tile(gqa_k_norm_g[j], GQA_KV_HEADS)]).reshape(1, qk_cols)
            qkv, xs = _proj(xs, mods, norm1_g[i], gqa_w_qkv[j].astype(BF16),
                            rope=rope, rope_cols=qk_cols, head_gain=gain, norm_cols=qk_cols, **common)
            o = _gqa_attn(qkv)
            w_o = gqa_w_o[j]
        else:
            proj, xs = _proj(xs, mods, norm1_g[i], conv_w_in[j].astype(BF16), **common)
            o = _conv_mix(proj, conv_w[j])
            w_o = conv_w_out[j]

        if has_ctx and not ctx_advances:
            tm_out, x_off, out_ctx_rows = CTX_LEN, CTX_LEN, 0
            if o.shape[1] == S_ALL:
                o = o[:, CTX_LEN:]
        elif has_ctx:
            tm_out, x_off, out_ctx_rows = 768, 0, CTX_LEN
        else:
            tm_out, x_off, out_ctx_rows = 1024, 0, 0
        xs, h_ext, route = _out_proj(o, xs, mods, norm2_g[i], w_o.astype(BF16), rw_split, rbias,
                                     tm=tm_out, ctx_rows=out_ctx_rows, x_row_off=x_off)
        s = xs.shape[1]
        pos, tile_group, n_active = _moe_plan(route, MOE_TM)
        y_sorted = _moe(h_ext.reshape(b * s, H_EXT), pos, tile_group, n_active, i, wg_all, wu_all, wd_all)
        moe_pending = (y_sorted, pos, mods)
    y_sorted, pos, mods = moe_pending
    s = xs.shape[1]
    return _combine_final(xs.reshape(b * s, D_MODEL), y_sorted, pos, mods, final_norm_g,
                          tm=1024, ctx_rows=0, s=s).reshape(b, s, D_MODEL)
```

```python
import functools
import math

import numpy as np
import jax
import jax.numpy as jnp
from jax import lax
from jax.experimental import pallas as pl
from jax.experimental.pallas import tpu as pltpu

D_MODEL = 1024
BATCH = 8
SEQ = 2048
DEPTH = 4
GRID_W = 64
CTX_LEN = 256
S_ALL = CTX_LEN + SEQ
N_MIXERS = 4
HEAD_DIM = 64
ATTN_SCALE = HEAD_DIM ** -0.5
LOG2E = math.log2(math.e)
QK_SCALE = ATTN_SCALE * LOG2E
DIFF_HEADS = D_MODEL // (2 * HEAD_DIM)
NA_HEADS = D_MODEL // HEAD_DIM
NA_WR = 8
NA_WC = 16
GQA_Q_HEADS = D_MODEL // HEAD_DIM
GQA_KV_HEADS = 4
ROPE_THETA = 10000.0
N_EXPERTS = 16
N_GROUPS = 4
GROUP_SIZE = N_EXPERTS // N_GROUPS
EXPERT_FF = 512
NORM_EPS = 1e-6
NEG_INF = -1e30

LANES = 128
SUBLANES = 8
MOD_ROWS = 16
CTX_MOD_ROW = BATCH
H_EXT = D_MODEL + LANES
ROUTE_ROWS = SUBLANES
MOE_TM = 512
ROW_ISSUE_UNROLL = 8
MAX_DMA_BURST = 64
ATTN_KCHUNK = 768
DIFF_TQ = 512
F32 = jnp.float32
BF16 = jnp.bfloat16
VMEM_LIMIT = 56 * 1024 * 1024

_NT = (((1,), (1,)), ((), ()))


def _cparams(n_axes):
    return pltpu.CompilerParams(dimension_semantics=("arbitrary",) * n_axes,
                                vmem_limit_bytes=VMEM_LIMIT)


def _dot(a, b):
    return jnp.dot(a, b, preferred_element_type=F32)


def _dot_nt(a, b):
    return lax.dot_general(a, b, _NT, preferred_element_type=F32)


def _sigmoid(x):
    return 1.0 / (1.0 + jnp.exp(-x))


def _row_mod(mb_ref, mc_ref, k, row0, tm, ctx_rows):
    mb = mb_ref[0, k:k + 1, :]
    if ctx_rows == 0:
        return mb
    rows = row0 + lax.broadcasted_iota(jnp.int32, (tm, 1), 0)
    return jnp.where(rows < ctx_rows, mc_ref[0, k:k + 1, :], mb)


def _rms(x, g):
    return x * lax.rsqrt(jnp.mean(x * x, axis=-1, keepdims=True) + NORM_EPS) * g


def _norm_mod(x, g, shift, scale):
    return _rms(x, g) * (1.0 + scale) + shift


def _row_copy(src_hbm, buf, sem, slot, src_row, dst_row):
    return pltpu.make_async_copy(src_hbm.at[pl.ds(src_row, 1)], buf.at[slot, pl.ds(dst_row, 1)], sem.at[slot])


def _wait_rows(src_hbm, buf, sem, slot, n):
    pltpu.make_async_copy(src_hbm.at[pl.ds(0, n)], buf.at[slot], sem.at[slot]).wait()


def _start_rows_loop(row_of, src_hbm, buf, sem, slot, n, two_queues=False):
    assert n % ROW_ISSUE_UNROLL == 0

    def body(t, carry):
        for k in range(ROW_ISSUE_UNROLL):
            j = t * ROW_ISSUE_UNROLL + k
            _row_copy(src_hbm, buf, sem, slot, row_of(j), j).start(priority=k % 2 if two_queues else 0)
        return carry

    lax.fori_loop(0, n // ROW_ISSUE_UNROLL, body, 0)


def _split_rows(n, parts):
    edges = [round(k * n / parts) for k in range(parts + 1)]
    return list(zip(edges[:-1], edges[1:]))


class _RowPrefetch:
    def __init__(self, row_of, src_hbm, buf, sem, slot, n, parts):
        self.args = (row_of, src_hbm, buf, sem, slot)
        self.pieces = _split_rows(n, parts)
        self.k = 0

    def issue_piece(self):
        row_of, src_hbm, buf, sem, slot = self.args
        lo, hi = self.pieces[self.k]
        self.k += 1
        for j in range(lo, hi):
            _row_copy(src_hbm, buf, sem, slot, row_of(j), j).start()


def _ada_kernel(cc_ref, w_ref, b_ref, o_ref):
    cc = cc_ref[...]
    s = (cc * _sigmoid(cc)).astype(BF16)
    o_ref[0] = _dot(s, w_ref[0].astype(BF16)) + b_ref[0]


def _ada(cc, ada_w, ada_b):
    tn = 1536
    n = 6 * D_MODEL
    return pl.pallas_call(
        _ada_kernel,
        grid=(DEPTH, n // tn),
        in_specs=[pl.BlockSpec((MOD_ROWS, D_MODEL), lambda l, j: (0, 0)),
                  pl.BlockSpec((1, D_MODEL, tn), lambda l, j: (l, 0, j)),
                  pl.BlockSpec((1, 1, tn), lambda l, j: (l, 0, j))],
        out_specs=pl.BlockSpec((1, MOD_ROWS, tn), lambda l, j: (l, 0, j)),
        out_shape=jax.ShapeDtypeStruct((DEPTH, MOD_ROWS, n), F32),
        compiler_params=_cparams(2), name="ada",
    )(cc, ada_w, ada_b.reshape(DEPTH, 1, n))


def _rope(x, cos, sin_signed):
    lane = lax.broadcasted_iota(jnp.int32, (1, LANES), 1)
    first_half = (lane % HEAD_DIM) < (HEAD_DIM // 2)
    partner = jnp.where(first_half, pltpu.roll(x, LANES - HEAD_DIM // 2, 1), pltpu.roll(x, HEAD_DIM // 2, 1))
    return x * cos + partner * sin_signed


def _proj_kernel(*refs, tm, tn, nout, ctx_rows, q_cols, rope_cols, norm_cols, fuse_moe, n_s, n_tiles):
    refs = list(refs)
    pos_ref = refs.pop(0) if fuse_moe else None
    x_ref, mb_ref, mc_ref, g_ref, w_ref = refs[:5]
    rest = refs[5:]
    cos_ref = sin_ref = hg_ref = seg_ref = y_hbm = pmb_ref = pmc_ref = None
    if rope_cols:
        cos_ref, sin_ref = rest[:2]
        rest = rest[2:]
    if norm_cols:
        hg_ref, seg_ref = rest[:2]
        rest = rest[2:]
    if fuse_moe:
        y_hbm, pmb_ref, pmc_ref = rest[:3]
        o_ref, xn_ref, buf, sem = rest[3:]
    else:
        (o_ref,) = rest

    row0 = pl.program_id(1) * tm
    x = x_ref[0]
    prefetch = None
    n_chunks = nout // tn
    pieces_per_chunk = -(-tm // (n_chunks * MAX_DMA_BURST))
    if fuse_moe:
        lin = pl.program_id(0) * n_s + pl.program_id(1)
        slot = lin % 2

        @pl.when(lin == 0)
        def _():
            _start_rows_loop(lambda j: pos_ref[j], y_hbm, buf, sem, 0, tm)

        _wait_rows(y_hbm, buf, sem, slot, tm)
        x = x + _row_mod(pmb_ref, pmc_ref, 5, row0, tm, ctx_rows) * buf[slot]
        xn_ref[0] = x
        nxt = jnp.minimum(lin + 1, n_tiles - 1) * tm
        prefetch = _RowPrefetch(lambda j: pos_ref[nxt + j], y_hbm, buf, sem, 1 - slot, tm, n_chunks * pieces_per_chunk)

    shift = _row_mod(mb_ref, mc_ref, 0, row0, tm, ctx_rows)
    scale = _row_mod(mb_ref, mc_ref, 1, row0, tm, ctx_rows)
    hb = _norm_mod(x, g_ref[...], shift, scale).astype(BF16)
    for c in range(n_chunks):
        lo = c * tn
        acc = _dot(hb, w_ref[:, lo:lo + tn])
        if prefetch is not None:
            prefetch.issue_piece()
        if lo < q_cols:
            acc = acc * QK_SCALE
        if lo < norm_cols:
            sq = acc * acc
            sq_hi = sq.astype(BF16)
            sq_lo = (sq - sq_hi.astype(F32)).astype(BF16)
            ss = _dot(sq_hi, seg_ref[...]) + _dot(sq_lo, seg_ref[...])
            acc = acc * lax.rsqrt(ss * (1.0 / HEAD_DIM) + NORM_EPS) * hg_ref[:, lo:lo + tn]
        if lo < rope_cols:
            cos, sin = cos_ref[...], sin_ref[...]
            acc = jnp.concatenate(
                [_rope(acc[:, k * LANES:(k + 1) * LANES], cos, sin) for k in range(tn // LANES)], axis=1)
        o_ref[0, :, lo:lo + tn] = acc.astype(BF16)
        if prefetch is not None:
            for _ in range(pieces_per_chunk - 1):
                prefetch.issue_piece()

    if fuse_moe:
        @pl.when(lin == n_tiles - 1)
        def _():
            _wait_rows(y_hbm, buf, sem, 1 - slot, tm)


def _proj(x, mods, g, w, *, tm, ctx_rows, q_cols=0, rope=None, rope_cols=0, head_gain=None, norm_cols=0, moe=None):
    b, s, d = x.shape
    nout = w.shape[1]
    tn = 256
    n_s = s // tm
    fuse = moe is not None
    assert s % tm == 0 and nout % tn == 0 and rope_cols % tn == 0 and norm_cols % tn == 0 and q_cols % tn == 0
    im = lambda f: (lambda bi, si, *_: f(bi, si))
    in_specs = [pl.BlockSpec((1, tm, d), im(lambda bi, si: (bi, si, 0))),
                pl.BlockSpec((1, 6, d), im(lambda bi, si: (bi, 0, 0))),
                pl.BlockSpec((1, 6, d), im(lambda bi, si: (CTX_MOD_ROW, 0, 0))),
                pl.BlockSpec((1, d), im(lambda bi, si: (0, 0))),
                pl.BlockSpec((d, nout), im(lambda bi, si: (0, 0)))]
    args = [x, mods, mods, g.reshape(1, d), w]
    if rope_cols:
        in_specs += [pl.BlockSpec((tm, LANES), im(lambda bi, si: (si, 0)))] * 2
        args += list(rope)
    if norm_cols:
        seg = np.kron(np.eye(tn // HEAD_DIM), np.ones((HEAD_DIM, HEAD_DIM))).astype(np.float32)
        in_specs += [pl.BlockSpec((1, norm_cols), im(lambda bi, si: (0, 0))),
                     pl.BlockSpec((tn, tn), im(lambda bi, si: (0, 0)))]
        args += [head_gain, jnp.asarray(seg, BF16)]
    out_specs = [pl.BlockSpec((1, tm, nout), im(lambda bi, si: (bi, si, 0)))]
    out_shape = [jax.ShapeDtypeStruct((b, s, nout), BF16)]
    scratch = []
    n_prefetch = 0
    if fuse:
        y_sorted, pos, prev_mods = moe
        in_specs += [pl.BlockSpec(memory_space=pl.ANY),
                     pl.BlockSpec((1, 6, d), im(lambda bi, si: (bi, 0, 0))),
                     pl.BlockSpec((1, 6, d), im(lambda bi, si: (CTX_MOD_ROW, 0, 0)))]
        args = [pos] + args + [y_sorted, prev_mods, prev_mods]
        out_specs.append(pl.BlockSpec((1, tm, d), im(lambda bi, si: (bi, si, 0))))
        out_shape.append(jax.ShapeDtypeStruct((b, s, d), F32))
        scratch = [pltpu.VMEM((2, tm, d), F32), pltpu.SemaphoreType.DMA((2,))]
        n_prefetch = 1
    kern = functools.partial(_proj_kernel, tm=tm, tn=tn, nout=nout, ctx_rows=ctx_rows, q_cols=q_cols,
                             rope_cols=rope_cols, norm_cols=norm_cols, fuse_moe=fuse, n_s=n_s, n_tiles=b * n_s)
    res = pl.pallas_call(
        kern,
        grid_spec=pltpu.PrefetchScalarGridSpec(num_scalar_prefetch=n_prefetch, grid=(b, n_s), in_specs=in_specs,
                                               out_specs=out_specs, scratch_shapes=scratch),
        out_shape=out_shape, compiler_params=_cparams(2), name="proj",
    )(*args)
    return (res[0], res[1]) if fuse else (res[0], x)


def _lane_halves():
    lane = lax.broadcasted_iota(jnp.int32, (1, LANES), 1)
    return lane < HEAD_DIM


def _head_half(x, lo, half, fill=0.0):
    f = jnp.full_like(x, fill)
    return jnp.where(lo, x, f) if half == 0 else jnp.where(lo, f, x)


def _exp2_parts(s_list):
    m = functools.reduce(jnp.maximum, [jnp.max(s, axis=-1, keepdims=True) for s in s_list])
    return [jnp.exp2(s - m) for s in s_list]


def _key_chunks(k_ref, v_ref, nk, chunk, v_map=lambda v: v):
    return [(k_ref[0, c0:min(c0 + chunk, nk)], v_map(v_ref[0, c0:min(c0 + chunk, nk)]), None)
            for c0 in range(0, nk, chunk)]


def _online_attend(q, chunks, row_sums=False):
    m = acc = l = None
    for k, v, bias in chunks:
        s = _dot_nt(q, k)
        if bias is not None:
            s = s + bias
        mc = jnp.max(s, axis=-1, keepdims=True)
        if acc is None:
            m = mc
            e = jnp.exp2(s - m)
            acc = _dot(e.astype(BF16), v)
            if row_sums:
                l = jnp.sum(e, axis=-1, keepdims=True)
        else:
            m_new = jnp.maximum(m, mc)
            alpha = jnp.exp2(m - m_new)
            e = jnp.exp2(s - m_new)
            acc = acc * alpha + _dot(e.astype(BF16), v)
            if row_sums:
                l = l * alpha + jnp.sum(e, axis=-1, keepdims=True)
            m = m_new
    return acc, l


def _pv_sum(o, half):
    l = o[:, HEAD_DIM:HEAD_DIM + 1] if half == 0 else o[:, 0:1]
    return o * (1.0 / l)


def _diff_attn_kernel(lam_ref, q_ref, k_ref, v_ref, g_ref, o_ref, *, lam_init, tq):
    lv = lam_ref[...]
    lam = (jnp.exp(jnp.sum(lv[0:1] * lv[1:2], axis=-1, keepdims=True))
           - jnp.exp(jnp.sum(lv[2:3] * lv[3:4], axis=-1, keepdims=True)) + lam_init)
    lo = _lane_halves()

    def attend(q, nk):
        nq = q.shape[0]
        q2 = jnp.concatenate([_head_half(q, lo, 0), _head_half(q, lo, 1)], axis=0)
        acc, l = _online_attend(q2, _key_chunks(k_ref, v_ref, nk, ATTN_KCHUNK), row_sums=True)
        o = acc[:nq] * (1.0 / l[:nq]) - acc[nq:] * (lam / l[nq:])
        return (_rms(o, g_ref[...]) * (1.0 - lam_init)).astype(BF16)

    o_ref[0, :CTX_LEN] = attend(q_ref[0, :CTX_LEN], CTX_LEN)

    def tile(t, carry):
        r0 = pl.multiple_of(CTX_LEN + t * tq, math.gcd(CTX_LEN, tq))
        o_ref[0, pl.ds(r0, tq)] = attend(q_ref[0, pl.ds(r0, tq)], S_ALL)
        return carry

    lax.fori_loop(0, SEQ // tq, tile, 0, unroll=4)


def _diff_attn(qkv, lam_vecs, subln_g, lam_init):
    b = qkv.shape[0]
    h = DIFF_HEADS
    kern = functools.partial(_diff_attn_kernel, lam_init=lam_init, tq=DIFF_TQ)
    slab = lambda off: pl.BlockSpec((1, S_ALL, LANES), lambda bi, hi: (bi, 0, off + hi))
    return pl.pallas_call(
        kern, grid=(b, h),
        in_specs=[pl.BlockSpec((4, HEAD_DIM), lambda bi, hi: (0, 0)), slab(0), slab(h), slab(2 * h),
                  pl.BlockSpec((1, LANES), lambda bi, hi: (0, 0))],
        out_specs=slab(0),
        out_shape=jax.ShapeDtypeStruct((b, S_ALL, D_MODEL), BF16),
        compiler_params=_cparams(2), name="diff_attn",
    )(lam_vecs, qkv, qkv, qkv, subln_g.reshape(1, LANES))


NA_ROWS = SEQ // GRID_W
NA_QROWS = 4
NA_QBLK = NA_QROWS * GRID_W
NA_BROWS = NA_QROWS + NA_WR
NA_BKEYS = NA_BROWS * GRID_W
NA_NBLK = NA_ROWS // NA_QROWS
NA_VARIANTS = 3
NA_BASE_ROWS = 2 * NA_WR


def _na_row_geometry():
    row_idx = np.zeros((NA_VARIANTS, NA_QROWS, NA_BROWS), np.int32)
    row_valid = np.zeros((NA_VARIANTS, NA_QROWS, NA_BROWS), bool)
    for var, rb in enumerate((0, 1, NA_NBLK - 1)):
        s0 = np.clip(NA_QROWS * rb - NA_WR // 2, 0, NA_ROWS - NA_BROWS)
        for a in range(NA_QROWS):
            r = NA_QROWS * rb + a
            rs = np.clip(r - NA_WR // 2, 0, NA_ROWS - NA_WR)
            for bb in range(NA_BROWS):
                kr = s0 + bb
                row_valid[var, a, bb] = rs <= kr < rs + NA_WR
                row_idx[var, a, bb] = kr - r + NA_WR - 1
    return row_idx, row_valid


def _na_build_bias(base_ref, bias_ref):
    row_idx, row_valid = _na_row_geometry()
    qc = lax.broadcasted_iota(jnp.int32, (GRID_W, LANES), 0)
    lane = lax.broadcasted_iota(jnp.int32, (GRID_W, LANES), 1)
    kc = lane % GRID_W
    win0 = jnp.clip(qc - NA_WC // 2, 0, GRID_W - NA_WC)
    col_valid = (kc >= win0) & (kc < win0 + NA_WC)
    low = lane < GRID_W
    for h in range(2):
        for var in range(NA_VARIANTS):
            for a in range(NA_QROWS):
                for bp in range(NA_BROWS // 2):
                    v0, v1 = row_valid[var, a, 2 * bp], row_valid[var, a, 2 * bp + 1]
                    rows = slice(a * GRID_W, (a + 1) * GRID_W)
                    cols = slice(bp * LANES, (bp + 1) * LANES)
                    if not (v0 or v1):
                        bias_ref[h, var, rows, cols] = jnp.full((GRID_W, LANES), NEG_INF, F32)
                        continue
                    k = int(row_idx[var, a, 2 * bp + 1])
                    slab = jnp.broadcast_to(base_ref[h, k:k + 1, :], (GRID_W, LANES))
                    slab = pltpu.roll(slab, LANES - (NA_WC - 1), 1, stride=1, stride_axis=0)
                    if v0 and v1:
                        ok = col_valid
                    elif v0:
                        ok = jnp.logical_and(col_valid, low)
                    else:
                        ok = jnp.logical_and(col_valid, jnp.logical_not(low))
                    bias_ref[h, var, rows, cols] = jnp.where(ok, slab, NEG_INF)


def _na_attn_kernel(q_ref, k_ref, v_ref, base_ref, o_ref, bias_ref):
    @pl.when(pl.program_id(1) == 0)
    def _():
        _na_build_bias(base_ref, bias_ref)

    lo = _lane_halves()
    kc = k_ref[0, :CTX_LEN]
    vc = v_ref[0, :CTX_LEN]
    vc_aug = [_head_half(vc, lo, h, 1.0) for h in range(2)]

    q = q_ref[0, :CTX_LEN]
    outs = []
    for h in range(2):
        (e,) = _exp2_parts([_dot_nt(_head_half(q, lo, h), kc)])
        outs.append(_pv_sum(_dot(e.astype(BF16), vc_aug[h]), h))
    o_ref[0, :CTX_LEN] = jnp.where(lo, outs[0], outs[1]).astype(BF16)

    def block(rb, carry):
        s0 = jnp.clip(NA_QROWS * rb - NA_WR // 2, 0, NA_ROWS - NA_BROWS)
        var = jnp.where(rb == 0, 0, jnp.where(rb == NA_NBLK - 1, 2, 1))
        q0 = pl.multiple_of(CTX_LEN + rb * NA_QBLK, NA_QBLK)
        b0 = pl.multiple_of(CTX_LEN + s0 * GRID_W, GRID_W)
        q = q_ref[0, pl.ds(q0, NA_QBLK)]
        kb = k_ref[0, pl.ds(b0, NA_BKEYS)]
        vb = v_ref[0, pl.ds(b0, NA_BKEYS)]
        q2 = jnp.concatenate([_head_half(q, lo, 0), _head_half(q, lo, 1)], axis=0)
        bias = jnp.concatenate([bias_ref[0, var], bias_ref[1, var]], axis=0)
        acc, l = _online_attend(q2, [(kb, vb, bias), (kc, vc, None)], row_sums=True)
        o2 = acc * (1.0 / l)
        o_ref[0, pl.ds(q0, NA_QBLK)] = jnp.where(lo, o2[:NA_QBLK], o2[NA_QBLK:]).astype(BF16)
        return carry

    lax.fori_loop(0, NA_NBLK, block, 0, unroll=4)


def _na_base_rows(rpb):
    n_c = 2 * NA_WC - 1
    r = rpb.astype(F32) * LOG2E
    zrow = jnp.zeros((NA_HEADS, 1, n_c), F32)
    low = jnp.concatenate([zrow, r], axis=1)
    high = jnp.concatenate([r, zrow], axis=1)
    pad = jnp.zeros((NA_HEADS, NA_BASE_ROWS, GRID_W - n_c), F32)
    return jnp.concatenate([low, pad, high, pad], axis=2)


def _na_attn(qkv, base):
    b = qkv.shape[0]
    hp = NA_HEADS // 2
    return pl.pallas_call(
        _na_attn_kernel, grid=(hp, b),
        in_specs=[pl.BlockSpec((1, S_ALL, LANES), lambda hi, bi: (bi, 0, hi)),
                  pl.BlockSpec((1, S_ALL, LANES), lambda hi, bi: (bi, 0, hp + hi)),
                  pl.BlockSpec((1, S_ALL, LANES), lambda hi, bi: (bi, 0, 2 * hp + hi)),
                  pl.BlockSpec((2, NA_BASE_ROWS, LANES), lambda hi, bi: (hi, 0, 0))],
        out_specs=pl.BlockSpec((1, S_ALL, LANES), lambda hi, bi: (bi, 0, hi)),
        out_shape=jax.ShapeDtypeStruct((b, S_ALL, D_MODEL), BF16),
        scratch_shapes=[pltpu.VMEM((2, NA_VARIANTS, NA_QBLK, NA_BKEYS), F32)],
        compiler_params=_cparams(2), name="na_attn",
    )(qkv, qkv, qkv, base)


GQA_GROUP = GQA_Q_HEADS // GQA_KV_HEADS


def _gqa_attn_kernel(q_ref, k_ref, v_ref, o_ref, *, tq):
    lo = _lane_halves()
    outs = [[None, None] for _ in range(GQA_GROUP)]
    for j in range(2):
        qs = []
        for g in range(GQA_GROUP):
            slab, half = divmod(j * GQA_GROUP + g, 2)
            q = q_ref[0, :, slab * LANES:(slab + 1) * LANES]
            if half != j:
                q = jnp.concatenate([q[:, HEAD_DIM:], q[:, :HEAD_DIM]], axis=1)
            qs.append(_head_half(q, lo, j))
        chunks = _key_chunks(k_ref, v_ref, S_ALL, ATTN_KCHUNK, v_map=lambda v: _head_half(v, lo, j, 1.0))
        acc, _ = _online_attend(jnp.concatenate(qs, axis=0), chunks)
        o = _pv_sum(acc, j)
        for g in range(GQA_GROUP):
            slab, half = divmod(j * GQA_GROUP + g, 2)
            og = o[g * tq:(g + 1) * tq]
            if half != j:
                og = pltpu.roll(og, HEAD_DIM, 1)
            outs[slab][half] = og
    for slab in range(GQA_GROUP):
        o_ref[0, :, slab * LANES:(slab + 1) * LANES] = jnp.where(lo, outs[slab][0], outs[slab][1]).astype(BF16)


def _gqa_attn(qkv):
    b = qkv.shape[0]
    tq = 256
    qw = GQA_GROUP * 2 * HEAD_DIM
    n_pairs = GQA_KV_HEADS // 2
    kv0 = GQA_Q_HEADS * HEAD_DIM // LANES
    kern = functools.partial(_gqa_attn_kernel, tq=tq)
    return pl.pallas_call(
        kern, grid=(b, n_pairs, SEQ // tq),
        in_specs=[pl.BlockSpec((1, tq, qw), lambda bi, pi, qi: (bi, qi + CTX_LEN // tq, pi)),
                  pl.BlockSpec((1, S_ALL, LANES), lambda bi, pi, qi: (bi, 0, kv0 + pi)),
                  pl.BlockSpec((1, S_ALL, LANES), lambda bi, pi, qi: (bi, 0, kv0 + n_pairs + pi))],
        out_specs=pl.BlockSpec((1, tq, qw), lambda bi, pi, qi: (bi, qi, pi)),
        out_shape=jax.ShapeDtypeStruct((b, SEQ, D_MODEL), BF16),
        compiler_params=_cparams(3), name="gqa_attn",
    )(qkv, qkv, qkv)


def _conv_kernel(gb_ref, gc_ref, u_ref, w_ref, o_ref):
    p = gc_ref[0].astype(F32) * u_ref[0].astype(F32)
    n = p.shape[0]
    rows = lax.broadcasted_iota(jnp.int32, (n, 1), 0)
    prev = jnp.where(rows == 0, 0.0, pltpu.roll(p, 1, 0))
    nxt = jnp.where(rows == n - 1, 0.0, pltpu.roll(p, n - 1, 0))
    w = w_ref[...]
    z = prev * w[0:1] + p * w[1:2] + nxt * w[2:3]
    o_ref[0] = (gb_ref[0].astype(F32) * z).astype(BF16)


def _conv_mix(proj, w_conv):
    b, n, _ = proj.shape
    nb = D_MODEL // LANES
    spec = lambda off: pl.BlockSpec((1, n, LANES), lambda bi, ci: (bi, 0, off + ci))
    return pl.pallas_call(
        _conv_kernel, grid=(b, nb),
        in_specs=[spec(0), spec(nb), spec(2 * nb), pl.BlockSpec((3, LANES), lambda bi, ci: (0, ci))],
        out_specs=pl.BlockSpec((1, n, LANES), lambda bi, ci: (bi, 0, ci)),
        out_shape=jax.ShapeDtypeStruct((b, n, D_MODEL), BF16),
        compiler_params=_cparams(2), name="conv_mix",
    )(proj, proj, proj, w_conv)


def _route(logits_t, bias):
    scores = _sigmoid(logits_t)
    sel = scores + bias
    nt = sel.shape[1]
    idx = lax.broadcasted_iota(jnp.int32, (N_EXPERTS, nt), 0).astype(F32)
    best = gidx = None
    for g in range(N_GROUPS):
        r = [sel[g * GROUP_SIZE + i:g * GROUP_SIZE + i + 1] for i in range(GROUP_SIZE)]
        pair_sums = [r[i] + r[j] for i in range(GROUP_SIZE) for j in range(i + 1, GROUP_SIZE)]
        gs = functools.reduce(jnp.maximum, pair_sums)
        if g == 0:
            best, gidx = gs, jnp.zeros_like(gs)
        else:
            upd = gs > best
            best = jnp.where(upd, gs, best)
            gidx = jnp.where(upd, float(g), gidx)
    in_group = jnp.floor(idx * (1.0 / GROUP_SIZE)) == gidx
    masked = jnp.where(in_group, sel, NEG_INF)
    m1 = jnp.max(masked, axis=0, keepdims=True)
    i1 = jnp.min(jnp.where(masked == m1, idx, float(N_EXPERTS)), axis=0, keepdims=True)
    masked2 = jnp.where(idx == i1, -jnp.inf, masked)
    m2 = jnp.max(masked2, axis=0, keepdims=True)
    i2 = jnp.min(jnp.where(masked2 == m2, idx, float(N_EXPERTS)), axis=0, keepdims=True)
    w1 = jnp.sum(jnp.where(idx == i1, scores, 0.0), axis=0, keepdims=True)
    w2 = jnp.sum(jnp.where(idx == i2, scores, 0.0), axis=0, keepdims=True)
    inv = 1.0 / (w1 + w2)
    gates = jnp.where(idx == i1, w1 * inv, 0.0) + jnp.where(idx == i2, w2 * inv, 0.0)
    rows = []
    for i in range(GROUP_SIZE):
        rows.append(functools.reduce(
            lambda a, b: a + b,
            [jnp.where(gidx == float(g), gates[g * GROUP_SIZE + i:g * GROUP_SIZE + i + 1], 0.0) for g in range(N_GROUPS)]))
    rows.append(gidx)
    rows += [jnp.zeros_like(gidx)] * (ROUTE_ROWS - len(rows))
    return jnp.concatenate(rows, axis=0)


def _out_kernel(o_ref, x_ref, mb_ref, mc_ref, g_ref, wo_ref, rw_ref, rb_ref,
                xn_ref, h_ref, rt_ref, *, tm, ctx_rows):
    row0 = pl.program_id(1) * tm
    mod = lambda k: _row_mod(mb_ref, mc_ref, k, row0, tm, ctx_rows)
    x = x_ref[0] + mod(2) * _dot(o_ref[0], wo_ref[...])
    xn_ref[0] = x
    h = _norm_mod(x, g_ref[...], mod(3), mod(4))
    hb = h.astype(BF16)
    hl = (h - hb.astype(F32)).astype(BF16)
    both = _dot_nt(rw_ref[...], hb)
    logits_t = both[:N_EXPERTS] + both[N_EXPERTS:] + _dot_nt(rw_ref[:N_EXPERTS], hl)
    route = _route(logits_t, rb_ref[...])
    rt_ref[0] = route
    h_ref[0, :, :D_MODEL] = h
    h_ref[0, :, D_MODEL:] = jnp.concatenate([route, jnp.zeros((LANES - ROUTE_ROWS, tm), F32)], axis=0).T


def _out_proj(o, x, mods, g, w_o, rw_split, rbias, *, tm, ctx_rows, x_row_off=0):
    b, s, d = o.shape
    assert s % tm == 0 and x_row_off % tm == 0
    xo = x_row_off // tm
    kern = functools.partial(_out_kernel, tm=tm, ctx_rows=ctx_rows)
    full = lambda shape: pl.BlockSpec(shape, lambda bi, si: (0,) * len(shape))
    return pl.pallas_call(
        kern, grid=(b, s // tm),
        in_specs=[pl.BlockSpec((1, tm, d), lambda bi, si: (bi, si, 0)),
                  pl.BlockSpec((1, tm, d), lambda bi, si: (bi, si + xo, 0)),
                  pl.BlockSpec((1, 6, d), lambda bi, si: (bi, 0, 0)),
                  pl.BlockSpec((1, 6, d), lambda bi, si: (CTX_MOD_ROW, 0, 0)),
                  full((1, d)), full((d, d)), full((2 * N_EXPERTS, d)), full((N_EXPERTS, 1))],
        out_specs=[pl.BlockSpec((1, tm, d), lambda bi, si: (bi, si, 0)),
                   pl.BlockSpec((1, tm, H_EXT), lambda bi, si: (bi, si, 0)),
                   pl.BlockSpec((1, ROUTE_ROWS, tm), lambda bi, si: (bi, 0, si))],
        out_shape=[jax.ShapeDtypeStruct((b, s, d), F32),
                   jax.ShapeDtypeStruct((b, s, H_EXT), F32),
                   jax.ShapeDtypeStruct((b, ROUTE_ROWS, s), F32)],
        compiler_params=_cparams(2), name="out_proj",
    )(o, x, mods, mods, g.reshape(1, d), w_o, rw_split, rbias)


def _moe_kernel(pos_ref, tgroup_ref, nact_ref, h_hbm, zeros_hbm, wg_ref, wu_ref, wd_ref, o_ref, perm, buf, sem, zsem,
                *, tm, n_tok):
    i = pl.program_id(0)
    n_act = nact_ref[0]

    @pl.when(i == 0)
    def _():
        clear = pltpu.make_async_copy(zeros_hbm, perm, zsem.at[0])
        clear.start()
        clear.wait()

        def inv(t, carry):
            perm[pos_ref[t]] = t
            return carry

        lax.fori_loop(0, n_tok, inv, 0, unroll=8)
        _start_rows_loop(lambda j: perm[j], h_hbm, buf, sem, 0, tm, two_queues=True)

    @pl.when(i < n_act)
    def _():
        slot = i % 2
        _wait_rows(h_hbm, buf, sem, slot, tm)
        @pl.when(i + 1 < n_act)
        def _():
            nxt = (i + 1) * tm
            _start_rows_loop(lambda j: perm[nxt + j], h_hbm, buf, sem, 1 - slot, tm, two_queues=True)

        hb = buf[slot, :, :D_MODEL].astype(BF16)
        gates = buf[slot, :, D_MODEL:]
        acc = jnp.zeros((tm, D_MODEL), F32)
        for j in range(GROUP_SIZE):
            g = _dot(hb, wg_ref[j].astype(BF16))
            u = _dot(hb, wu_ref[j].astype(BF16))
            a = (g * _sigmoid(g)) * u * gates[:, j:j + 1]
            acc = acc + _dot(a.astype(BF16), wd_ref[j].astype(BF16))
        o_ref[...] = acc

    @pl.when(i >= n_act)
    def _():
        o_ref[...] = jnp.zeros_like(o_ref)


def _moe(h_ext, pos, tile_group, n_active, layer, wg, wu, wd):
    tm = MOE_TM
    n_tok = pos.shape[0]
    n_tiles = tile_group.shape[0]
    d = D_MODEL
    wspec = lambda shape: pl.BlockSpec((None,) + shape, lambda i, ps, tg, na: (layer, tg[i], 0, 0),
                                       pipeline_mode=pl.Buffered(1))
    kern = functools.partial(_moe_kernel, tm=tm, n_tok=n_tok)
    return pl.pallas_call(
        kern,
        grid_spec=pltpu.PrefetchScalarGridSpec(
            num_scalar_prefetch=3, grid=(n_tiles,),
            in_specs=[pl.BlockSpec(memory_space=pl.ANY), pl.BlockSpec(memory_space=pl.ANY),
                      wspec((GROUP_SIZE, d, EXPERT_FF)), wspec((GROUP_SIZE, d, EXPERT_FF)),
                      wspec((GROUP_SIZE, EXPERT_FF, d))],
            out_specs=pl.BlockSpec((tm, d), lambda i, *_: (i, 0)),
            scratch_shapes=[pltpu.SMEM((n_tiles * tm,), jnp.int32), pltpu.VMEM((2, tm, H_EXT), F32),
                            pltpu.SemaphoreType.DMA((2,)), pltpu.SemaphoreType.DMA((1,))]),
        out_shape=jax.ShapeDtypeStruct((n_tiles * tm, d), F32),
        compiler_params=_cparams(1), name="moe",
    )(pos, tile_group, n_active, h_ext, jnp.zeros((n_tiles * tm,), jnp.int32), wg, wu, wd)


def _moe_plan(route, tm):
    b, _, s = route.shape
    t = b * s
    n_tiles = t // tm + N_GROUPS
    gid = route[:, GROUP_SIZE, :].reshape(t).astype(jnp.int32)
    onehot = (gid[:, None] == jnp.arange(N_GROUPS, dtype=jnp.int32)[None, :]).astype(jnp.int32)
    csum = jnp.cumsum(onehot, axis=0)
    counts = csum[-1]
    in_group_rank = jnp.sum((csum - onehot) * onehot, axis=1)
    tiles = (counts + tm - 1) // tm
    tile_end = jnp.cumsum(tiles)
    tile_start = tile_end - tiles
    pos = jnp.sum(onehot * tile_start[None, :], axis=1) * tm + in_group_rank
    tile_ids = jnp.arange(n_tiles, dtype=jnp.int32)
    tile_group = jnp.minimum(jnp.sum((tile_ids[:, None] >= tile_end[None, :]).astype(jnp.int32), axis=1), N_GROUPS - 1)
    i32 = lambda a: a.astype(jnp.int32)
    return i32(pos), i32(tile_group), i32(tile_end[-1:])


def _combine_kernel(pos_ref, x_ref, y_hbm, mb_ref, mc_ref, fg_ref, o_ref, buf, sem, *, tm, ctx_rows, tiles_per_sample, n_tiles):
    i = pl.program_id(0)
    slot = i % 2

    @pl.when(i == 0)
    def _():
        _start_rows_loop(lambda j: pos_ref[j], y_hbm, buf, sem, 0, tm, two_queues=True)

    @pl.when(i + 1 < n_tiles)
    def _():
        nxt = (i + 1) * tm
        _start_rows_loop(lambda j: pos_ref[nxt + j], y_hbm, buf, sem, 1 - slot, tm, two_queues=True)

    _wait_rows(y_hbm, buf, sem, slot, tm)
    row0 = (i % tiles_per_sample) * tm
    x = x_ref[...] + _row_mod(mb_ref, mc_ref, 5, row0, tm, ctx_rows) * buf[slot]
    o_ref[...] = _rms(x, fg_ref[...])


def _combine_final(x, y_sorted, pos, mods, final_g, *, tm, ctx_rows, s):
    t, d = x.shape
    tps = s // tm
    n_tiles = t // tm
    kern = functools.partial(_combine_kernel, tm=tm, ctx_rows=ctx_rows, tiles_per_sample=tps, n_tiles=n_tiles)
    return pl.pallas_call(
        kern,
        grid_spec=pltpu.PrefetchScalarGridSpec(
            num_scalar_prefetch=1, grid=(n_tiles,),
            in_specs=[pl.BlockSpec((tm, d), lambda i, pos: (i, 0)),
                      pl.BlockSpec(memory_space=pl.ANY),
                      pl.BlockSpec((1, 6, d), lambda i, pos: (i // tps, 0, 0)),
                      pl.BlockSpec((1, 6, d), lambda i, pos: (CTX_MOD_ROW, 0, 0)),
                      pl.BlockSpec((1, d), lambda i, pos: (0, 0))],
            out_specs=pl.BlockSpec((tm, d), lambda i, pos: (i, 0)),
            scratch_shapes=[pltpu.VMEM((2, tm, d), F32), pltpu.SemaphoreType.DMA((2,))]),
        out_shape=jax.ShapeDtypeStruct((t, d), F32),
        compiler_params=_cparams(1), name="combine",
    )(pos, x, y_sorted, mods, mods, final_g.reshape(1, d))


def _rope_tables():
    t = jnp.arange(SEQ)
    row = (t // GRID_W).astype(F32)
    col = (t % GRID_W).astype(F32)
    quarter = HEAD_DIM // 4
    inv_freq = ROPE_THETA ** (-jnp.arange(quarter, dtype=F32) / quarter)
    ang = jnp.concatenate([row[:, None] * inv_freq, col[:, None] * inv_freq], axis=-1)
    cos, sin = jnp.cos(ang), jnp.sin(ang)
    reps = LANES // HEAD_DIM
    cos_l = jnp.tile(jnp.concatenate([cos, cos], axis=-1), (1, reps))
    sin_l = jnp.tile(jnp.concatenate([-sin, sin], axis=-1), (1, reps))
    cos_all = jnp.concatenate([jnp.ones((CTX_LEN, LANES), F32), cos_l], axis=0)
    sin_all = jnp.concatenate([jnp.zeros((CTX_LEN, LANES), F32), sin_l], axis=0)
    return cos_all, sin_all


def kernel(x, c, ctx, c_ctx, ada_w, ada_b, norm1_g, norm2_g, final_norm_g, diff_w_qkv, diff_w_o, diff_lambda_q1, diff_lambda_k1, diff_lambda_q2, diff_lambda_k2, diff_subln_g, na_w_qkv, na_w_o, na_rpb, gqa_w_qkv, gqa_w_o, gqa_q_norm_g, gqa_k_norm_g, conv_w_in, conv_w, conv_w_out, router_w, router_bias, moe_w_gate, moe_w_up, moe_w_down):
    assert DEPTH % N_MIXERS == 0
    b = x.shape[0]
    cc = jnp.zeros((MOD_ROWS, D_MODEL), F32).at[:b].set(c).at[CTX_MOD_ROW].set(c_ctx)
    mods_all = _ada(cc, ada_w, ada_b).reshape(DEPTH, MOD_ROWS, 6, D_MODEL)
    rope = _rope_tables()
    rw_t = router_w.T
    rw_hi = rw_t.astype(BF16)
    rw_split = jnp.concatenate([rw_hi, (rw_t - rw_hi.astype(F32)).astype(BF16)], axis=0)
    rbias = router_bias.reshape(N_EXPERTS, 1).astype(F32)
    wg_all, wu_all, wd_all = moe_w_gate, moe_w_up, moe_w_down

    xs = jnp.concatenate([ctx, x], axis=1)
    moe_pending = None
    for i in range(DEPTH):
        m, j = i % N_MIXERS, i // N_MIXERS
        mods = mods_all[i]
        has_ctx = xs.shape[1] == S_ALL
        ctx_rows = CTX_LEN if has_ctx else 0
        ctx_advances = any(k % N_MIXERS != 3 for k in range(i + 1, DEPTH))
        common = dict(tm=768 if has_ctx else 512, ctx_rows=ctx_rows, moe=moe_pending)
        if m == 0:
            lam_init = 0.8 - 0.6 * math.exp(-0.3 * i)
            qkv, xs = _proj(xs, mods, norm1_g[i], diff_w_qkv[j].astype(BF16), q_cols=D_MODEL,
                            rope=rope, rope_cols=2 * D_MODEL, **common)
            lam_vecs = jnp.stack([diff_lambda_q1[j], diff_lambda_k1[j], diff_lambda_q2[j], diff_lambda_k2[j]])
            o = _diff_attn(qkv, lam_vecs, diff_subln_g[j], lam_init)
            w_o = diff_w_o[j]
        elif m == 1:
            qkv, xs = _proj(xs, mods, norm1_g[i], na_w_qkv[j].astype(BF16), q_cols=D_MODEL, **common)
            o = _na_attn(qkv, _na_base_rows(na_rpb[j]))
            w_o = na_w_o[j]
        elif m == 2:
            qk_cols = (GQA_Q_HEADS + GQA_KV_HEADS) * HEAD_DIM
            gain = jnp.concatenate([jnp.tile(gqa_q_norm_g[j] * QK_SCALE, GQA_Q_HEADS),
                                    jnp.tile(gqa_k_norm_g[j], GQA_KV_HEADS)]).reshape(1, qk_cols)
            qkv, xs = _proj(xs, mods, norm1_g[i], gqa_w_qkv[j].astype(BF16),
                            rope=rope, rope_cols=qk_cols, head_gain=gain, norm_cols=qk_cols, **common)
            o = _gqa_attn(qkv)
            w_o = gqa_w_o[j]
        else:
            proj, xs = _proj(xs, mods, norm1_g[i], conv_w_in[j].astype(BF16), **common)
            o = _conv_mix(proj, conv_w[j])
            w_o = conv_w_out[j]

        if has_ctx and not ctx_advances:
            tm_out, x_off, out_ctx_rows = CTX_LEN, CTX_LEN, 0
            if o.shape[1] == S_ALL:
                o = o[:, CTX_LEN:]
        elif has_ctx:
            tm_out, x_off, out_ctx_rows = 768, 0, CTX_LEN
        else:
            tm_out, x_off, out_ctx_rows = 1024, 0, 0
        xs, h_ext, route = _out_proj(o, xs, mods, norm2_g[i], w_o.astype(BF16), rw_split, rbias,
                                     tm=tm_out, ctx_rows=out_ctx_rows, x_row_off=x_off)
        s = xs.shape[1]
        pos, tile_group, n_active = _moe_plan(route, MOE_TM)
        y_sorted = _moe(h_ext.reshape(b * s, H_EXT), pos, tile_group, n_active, i, wg_all, wu_all, wd_all)
        moe_pending = (y_sorted, pos, mods)
    y_sorted, pos, mods = moe_pending
    s = xs.shape[1]
    return _combine_final(xs.reshape(b * s, D_MODEL), y_sorted, pos, mods, final_norm_g,
                          tm=1024, ctx_rows=0, s=s).reshape(b, s, D_MODEL)
```
